```python
import math, functools
import jax, jax.numpy as jnp
from jax import lax
import numpy as np

D_MODEL = 1024
BATCH = 2
SEQ = 8192
DEPTH = 1
DEC_BATCH = 128
DEC_SEQ = 1
PAST_LEN = 8192
PAGE_SIZE = 128

HEAD_DIM = 64
N_FOX_HEADS = 8
N_DIFF_HEADS = 4
FOX_W = N_FOX_HEADS * HEAD_DIM
DIFF_V_DIM = 2 * HEAD_DIM
DIFF_QK_W = N_DIFF_HEADS * 2 * HEAD_DIM
DIFF_V_W = N_DIFF_HEADS * DIFF_V_DIM
GATE_W = 2 * D_MODEL
IN_SIZES = (FOX_W, FOX_W, FOX_W, N_FOX_HEADS, DIFF_QK_W, DIFF_QK_W, DIFF_V_W, GATE_W)
IN_W = FOX_W * 3 + N_FOX_HEADS + DIFF_QK_W * 2 + DIFF_V_W + GATE_W
N_EXPERTS = 32
TOP_K = 4
D_FF = D_MODEL
SWIGLU_LIMIT = 7.0
SWIGLU_ALPHA = 1.702
MOE_BLOCK = 128
Q_BLOCK = 128
EPS = 1e-5
FORGET_BIAS = 2.0
POOL_NUM = 5
POOL_DEN = 4

kernel_name = 'fox_diffattn_gated_moe_step'


def rmsnorm(x, g):
    xf = x.astype(jnp.float32)
    y = xf * lax.rsqrt(jnp.mean(xf * xf, axis=-1, keepdims=True) + EPS)
    return (y * g.astype(jnp.float32)).astype(x.dtype)


def alibi_slopes(n_heads):
    return 2.0 ** (-8.0 * jnp.arange(1, n_heads + 1, dtype=jnp.float32) / n_heads)


def split_projection(h, w_in, b_forget, b_gate):
    b, t = h.shape[:2]
    points, acc = [], 0
    for size in IN_SIZES[:-1]:
        acc += size
        points.append(acc)
    fq, fk, fv, zf, dq, dk, dv, zg = jnp.split(h @ w_in, points, axis=-1)
    fq = fq.reshape(b, t, N_FOX_HEADS, HEAD_DIM)
    fk = fk.reshape(b, t, N_FOX_HEADS, HEAD_DIM)
    fv = fv.reshape(b, t, N_FOX_HEADS, HEAD_DIM)
    logf = jax.nn.log_sigmoid((zf + b_forget).astype(jnp.float32))
    dq = dq.reshape(b, t, N_DIFF_HEADS, 2, HEAD_DIM)
    dk = dk.reshape(b, t, N_DIFF_HEADS, 2, HEAD_DIM)
    dv = dv.reshape(b, t, N_DIFF_HEADS, DIFF_V_DIM)
    g_fox, g_diff = jnp.split(jax.nn.sigmoid(zg + b_gate), 2, axis=-1)
    return fq, fk, fv, logf, dq, dk, dv, g_fox, g_diff


def fox_attend(q, k, v, cq, ck, q_pos, k_pos):
    s = jnp.einsum('bqhd,bkhd->bhqk', q, k).astype(jnp.float32) * (HEAD_DIM ** -0.5)
    s = s + jnp.swapaxes(cq, 1, 2)[:, :, :, None] - jnp.swapaxes(ck, 1, 2)[:, :, None, :]
    s = jnp.where(k_pos[None, :] <= q_pos[:, None], s, -jnp.inf)
    p = jax.nn.softmax(s, axis=-1)
    return jnp.einsum('bhqk,bkhd->bqhd', p.astype(v.dtype), v)


def diff_attend(q, k, v, q_pos, k_pos, slopes, lam, lam_init, g_subln):
    s = jnp.einsum('bqhid,bkhid->bhiqk', q, k).astype(jnp.float32) * (HEAD_DIM ** -0.5)
    dist = (q_pos[:, None] - k_pos[None, :]).astype(jnp.float32)
    s = s - slopes[None, :, None, None, None] * dist
    s = jnp.where(k_pos[None, :] <= q_pos[:, None], s, -jnp.inf)
    p = jax.nn.softmax(s, axis=-1)
    a = p[:, :, 0] - lam * p[:, :, 1]
    o = jnp.einsum('bhqk,bkhe->bqhe', a.astype(v.dtype), v)
    return rmsnorm(o, g_subln) * (1.0 - lam_init)


def prompt_attention(fq, fk, fv, logf, dq, dk, dv, slopes, lam, lam_init, g_subln):
    b, t = fq.shape[:2]
    n_blocks = t // Q_BLOCK
    c = jnp.cumsum(logf, axis=1)
    k_pos = jnp.arange(t)

    def block(i):
        start = i * Q_BLOCK
        q_pos = start + jnp.arange(Q_BLOCK)
        sl = lambda a: lax.dynamic_slice_in_dim(a, start, Q_BLOCK, axis=1)
        of = fox_attend(sl(fq), fk, fv, sl(c), c, q_pos, k_pos)
        od = diff_attend(sl(dq), dk, dv, q_pos, k_pos, slopes, lam, lam_init, g_subln)
        return of, od

    of, od = lax.map(block, jnp.arange(n_blocks))
    of = jnp.moveaxis(of, 0, 1).reshape(b, t, N_FOX_HEADS, HEAD_DIM)
    od = jnp.moveaxis(od, 0, 1).reshape(b, t, N_DIFF_HEADS, DIFF_V_DIM)
    return of, od


def sample_attention(fq, fk, fv, logf, dq, dk, dv, layer, page_table, cache_fox_k, cache_fox_v,
                     cache_fox_logf, cache_diff_k, cache_diff_v, slopes, lam, lam_init, g_subln):
    s_new = fq.shape[1]
    past = page_table.shape[1] * PAGE_SIZE
    q_pos = past + jnp.arange(s_new)
    k_pos = jnp.arange(past + s_new)

    def gather(cache, pages):
        rows = cache[layer, pages]
        return rows.reshape((past,) + rows.shape[2:])

    def one_seq(args):
        fq1, fk1, fv1, lf1, dq1, dk1, dv1, pages = args
        fk_all = jnp.concatenate([gather(cache_fox_k, pages), fk1], axis=0)[None]
        fv_all = jnp.concatenate([gather(cache_fox_v, pages), fv1], axis=0)[None]
        lf_all = jnp.concatenate([gather(cache_fox_logf, pages).astype(jnp.float32), lf1], axis=0)
        c = jnp.cumsum(lf_all, axis=0)[None]
        dk_all = jnp.concatenate([gather(cache_diff_k, pages), dk1], axis=0)[None]
        dv_all = jnp.concatenate([gather(cache_diff_v, pages), dv1], axis=0)[None]
        of = fox_attend(fq1[None], fk_all, fv_all, c[:, past:], c, q_pos, k_pos)
        od = diff_attend(dq1[None], dk_all, dv_all, q_pos, k_pos, slopes, lam, lam_init, g_subln)
        return of[0], od[0]

    return lax.map(one_seq, (fq, fk, fv, logf, dq, dk, dv, page_table))


def mixer_sublayer(x, attend, g_norm, w_in, b_forget, b_gate, w_up_fox, w_up_diff, w_out):
    b, t = x.shape[:2]
    h = rmsnorm(x, g_norm)
    fq, fk, fv, lf, dq, dk, dv, g_fox, g_diff = split_projection(h, w_in, b_forget, b_gate)
    of, od = attend(fq, fk, fv, lf, dq, dk, dv)
    merged = (g_fox * (of.reshape(b, t, FOX_W) @ w_up_fox)
              + g_diff * (od.reshape(b, t, DIFF_V_W) @ w_up_diff))
    return x + merged @ w_out, (fk, fv, lf, dk, dv)


def clamped_swiglu(gu):
    glu, lin = gu[..., :D_FF], gu[..., D_FF:]
    glu = jnp.minimum(glu, SWIGLU_LIMIT)
    lin = jnp.clip(lin, -SWIGLU_LIMIT, SWIGLU_LIMIT)
    return glu * jax.nn.sigmoid(SWIGLU_ALPHA * glu) * (lin + 1.0)


def moe(h, w_router, b_router, w_gate_up, b_gate_up, w_down, b_down):
    n_tok, d = h.shape
    logits = (h @ w_router).astype(jnp.float32) + b_router.astype(jnp.float32)
    top_vals, top_idx = lax.top_k(logits, TOP_K)
    weights = jax.nn.softmax(top_vals, axis=-1)
    n_assign = n_tok * TOP_K
    flat_e = top_idx.reshape(-1)
    order = jnp.argsort(flat_e)
    sorted_e = flat_e[order]
    sorted_tok = order // TOP_K
    counts = jnp.zeros((N_EXPERTS,), jnp.int32).at[flat_e].add(1)
    padded = (counts + MOE_BLOCK - 1) // MOE_BLOCK * MOE_BLOCK
    pad_end = jnp.cumsum(padded)
    pad_start = pad_end - padded
    start = jnp.cumsum(counts) - counts
    dest = pad_start[sorted_e] + (jnp.arange(n_assign) - start[sorted_e])
    n_blocks = -(-n_assign // MOE_BLOCK) + N_EXPERTS
    rows_tok = jnp.zeros((n_blocks * MOE_BLOCK,), jnp.int32).at[dest].set(sorted_tok)
    block_expert = jnp.minimum(
        jnp.searchsorted(pad_end, jnp.arange(n_blocks) * MOE_BLOCK, side='right'), N_EXPERTS - 1)
    x_rows = h[rows_tok].reshape(n_blocks, MOE_BLOCK, d)

    def expert_block(args):
        xb, e = args
        gu = xb @ w_gate_up[e] + b_gate_up[e]
        return clamped_swiglu(gu) @ w_down[e] + b_down[e]

    y_rows = lax.map(expert_block, (x_rows, block_expert)).reshape(-1, d)
    y_assign = y_rows[dest] * weights.reshape(-1)[order][:, None].astype(h.dtype)
    return jnp.zeros_like(h).at[sorted_tok].add(y_assign)


def ffn_sublayer(x, g_norm, w_router, b_router, w_gate_up, b_gate_up, w_down, b_down):
    b, t, d = x.shape
    h = rmsnorm(x, g_norm).reshape(b * t, d)
    return x + moe(h, w_router, b_router, w_gate_up, b_gate_up, w_down, b_down).reshape(b, t, d)


def setup_inputs(seed: int = 0) -> dict:
    key = jax.random.key(seed)
    ks = jax.random.split(key, 28)
    n_pages = PAST_LEN // PAGE_SIZE
    n_phys = (DEC_BATCH * n_pages * POOL_NUM) // POOL_DEN
    nrm = lambda k, shape, scale: jax.random.normal(k, shape, jnp.float32) * scale
    page_table = jax.random.permutation(ks[7], n_phys)[: DEC_BATCH * n_pages]
    return {
        'x_prompt': nrm(ks[0], (BATCH, SEQ, D_MODEL), 1.0),
        'x_sample': nrm(ks[1], (DEC_BATCH, DEC_SEQ, D_MODEL), 1.0),
        'cache_fox_k': nrm(ks[2], (DEPTH, n_phys, PAGE_SIZE, N_FOX_HEADS, HEAD_DIM), 1.0),
        'cache_fox_v': nrm(ks[3], (DEPTH, n_phys, PAGE_SIZE, N_FOX_HEADS, HEAD_DIM), 1.0),
        'cache_fox_logf': jax.nn.log_sigmoid(
            FORGET_BIAS + nrm(ks[4], (DEPTH, n_phys, PAGE_SIZE, N_FOX_HEADS), 1.0)),
        'cache_diff_k': nrm(ks[5], (DEPTH, n_phys, PAGE_SIZE, N_DIFF_HEADS, 2, HEAD_DIM), 1.0),
        'cache_diff_v': nrm(ks[6], (DEPTH, n_phys, PAGE_SIZE, N_DIFF_HEADS, DIFF_V_DIM), 1.0),
        'page_table': page_table.reshape(DEC_BATCH, n_pages).astype(jnp.int32),
        'g_attn_norm': 1.0 + nrm(ks[8], (DEPTH, D_MODEL), 0.02),
        'w_in': nrm(ks[9], (DEPTH, D_MODEL, IN_W), D_MODEL ** -0.5),
        'b_forget': FORGET_BIAS + nrm(ks[10], (DEPTH, N_FOX_HEADS), 0.3),
        'b_gate': nrm(ks[11], (DEPTH, GATE_W), 0.02),
        'lambda_q1': nrm(ks[12], (DEPTH, HEAD_DIM), 0.1),
        'lambda_k1': nrm(ks[13], (DEPTH, HEAD_DIM), 0.1),
        'lambda_q2': nrm(ks[14], (DEPTH, HEAD_DIM), 0.1),
        'lambda_k2': nrm(ks[15], (DEPTH, HEAD_DIM), 0.1),
        'g_subln': 1.0 + nrm(ks[16], (DEPTH, DIFF_V_DIM), 0.02),
        'w_up_fox': nrm(ks[17], (DEPTH, FOX_W, D_MODEL), FOX_W ** -0.5),
        'w_up_diff': nrm(ks[18], (DEPTH, DIFF_V_W, D_MODEL), DIFF_V_W ** -0.5),
        'w_out': nrm(ks[19], (DEPTH, D_MODEL, D_MODEL), D_MODEL ** -0.5),
        'g_ffn_norm': 1.0 + nrm(ks[20], (DEPTH, D_MODEL), 0.02),
        'w_router': nrm(ks[21], (DEPTH, D_MODEL, N_EXPERTS), D_MODEL ** -0.5),
        'b_router': nrm(ks[22], (DEPTH, N_EXPERTS), 0.01),
        'w_gate_up': nrm(ks[23], (DEPTH, N_EXPERTS, D_MODEL, 2 * D_FF), D_MODEL ** -0.5),
        'b_gate_up': nrm(ks[24], (DEPTH, N_EXPERTS, 2 * D_FF), 0.01),
        'w_down': nrm(ks[25], (DEPTH, N_EXPERTS, D_FF, D_MODEL), D_FF ** -0.5),
        'b_down': nrm(ks[26], (DEPTH, N_EXPERTS, D_MODEL), 0.01),
        'g_final': 1.0 + nrm(ks[27], (D_MODEL,), 0.02),
    }


def reference(x_prompt, x_sample, cache_fox_k, cache_fox_v, cache_fox_logf, cache_diff_k, cache_diff_v,
              page_table, g_attn_norm, w_in, b_forget, b_gate, lambda_q1, lambda_k1, lambda_q2, lambda_k2,
              g_subln, w_up_fox, w_up_diff, w_out, g_ffn_norm, w_router, b_router, w_gate_up, b_gate_up,
              w_down, b_down, g_final):
    slopes = alibi_slopes(N_DIFF_HEADS)
    xp, xs = x_prompt, x_sample
    prompt_rows, sample_rows = [], []
    for l in range(DEPTH):
        lam_init = 0.8 - 0.6 * math.exp(-0.3 * l)
        lam = (jnp.exp(jnp.sum(lambda_q1[l].astype(jnp.float32) * lambda_k1[l].astype(jnp.float32)))
               - jnp.exp(jnp.sum(lambda_q2[l].astype(jnp.float32) * lambda_k2[l].astype(jnp.float32)))
               + lam_init)
        mix = (g_attn_norm[l], w_in[l], b_forget[l], b_gate[l], w_up_fox[l], w_up_diff[l], w_out[l])
        ffn = (g_ffn_norm[l], w_router[l], b_router[l], w_gate_up[l], b_gate_up[l], w_down[l], b_down[l])
        attend_p = functools.partial(prompt_attention, slopes=slopes, lam=lam, lam_init=lam_init,
                                     g_subln=g_subln[l])
        attend_s = functools.partial(sample_attention, layer=l, page_table=page_table,
                                     cache_fox_k=cache_fox_k, cache_fox_v=cache_fox_v,
                                     cache_fox_logf=cache_fox_logf, cache_diff_k=cache_diff_k,
                                     cache_diff_v=cache_diff_v, slopes=slopes, lam=lam,
                                     lam_init=lam_init, g_subln=g_subln[l])
        xp, rows_p = mixer_sublayer(xp, attend_p, *mix)
        xp = ffn_sublayer(xp, *ffn)
        xs, rows_s = mixer_sublayer(xs, attend_s, *mix)
        xs = ffn_sublayer(xs, *ffn)
        prompt_rows.append(rows_p)
        sample_rows.append(rows_s)
    new_fox_k_prompt, new_fox_v_prompt, new_fox_logf_prompt, new_diff_k_prompt, new_diff_v_prompt = [
        jnp.stack(r) for r in zip(*prompt_rows)]
    new_fox_k_sample, new_fox_v_sample, new_fox_logf_sample, new_diff_k_sample, new_diff_v_sample = [
        jnp.stack(r) for r in zip(*sample_rows)]
    y_prompt = rmsnorm(xp, g_final)
    y_sample = rmsnorm(xs, g_final)
    return (y_prompt, y_sample, new_fox_k_prompt, new_fox_v_prompt, new_fox_logf_prompt, new_diff_k_prompt,
            new_diff_v_prompt, new_fox_k_sample, new_fox_v_sample, new_fox_logf_sample, new_diff_k_sample,
            new_diff_v_sample)
```

```python
import functools
import math

import jax
import jax.numpy as jnp
from jax import lax
from jax.experimental import pallas as pl
from jax.experimental.pallas import tpu as pltpu

F32 = jnp.float32
BF16 = jnp.bfloat16
I32 = jnp.int32

HEAD_DIM = 64
LANES = 128
EPS = 1e-5
SWIGLU_LIMIT = 7.0
SWIGLU_ALPHA = 1.702
TOP_K = 4
ROW_BLOCK = 128
VMEM_LIMIT = 56 * 1024 * 1024
NEG_INF = float("-inf")
HI = lax.Precision.HIGHEST


def _cparams(n_axes, **kw):
    return pltpu.CompilerParams(dimension_semantics=("arbitrary",) * n_axes,
                                vmem_limit_bytes=VMEM_LIMIT, **kw)


def _const_spec(shape):
    nd = len(shape)
    return pl.BlockSpec(shape, lambda *_: (0,) * nd, pipeline_mode=pl.Buffered(1))


def _trunc_bf16(x):
    bits = pltpu.bitcast(x, jnp.uint32) & jnp.uint32(0xFFFF0000)
    return pltpu.bitcast(bits, F32)


def _split3(x):
    hi = _trunc_bf16(x)
    r = x - hi
    mid = _trunc_bf16(r)
    lo = r - mid
    return hi.astype(BF16), mid.astype(BF16), lo.astype(BF16)


def _split3_xla(x):
    def trunc(v):
        bits = lax.bitcast_convert_type(v, jnp.uint32) & jnp.uint32(0xFFFF0000)
        return lax.bitcast_convert_type(bits, F32)
    hi = trunc(x)
    r = x - hi
    mid = trunc(r)
    lo = r - mid
    return hi, mid, lo


def _rms(x, g):
    ms = jnp.mean(x * x, axis=-1, keepdims=True)
    return x * lax.rsqrt(ms + EPS) * g


def _tri_dot(tri_bf16, x):
    hi, mid, lo = _split3(x)
    d = lambda p: jnp.dot(tri_bf16, p, preferred_element_type=F32)
    return d(hi) + d(mid) + d(lo)


def _proj_kernel(x_ref, g_ref, w_ref, wf_ref, bf_ref, bg_ref,
                 q_ref, k_ref, v_ref, fk_ref, fv_ref, dk_ref, dv_ref, lf_ref, c_ref, gate_ref,
                 carry_ref, *, blocks_per_seq, width):
    i = pl.program_id(0)
    tm = x_ref.shape[0]
    h = _rms(x_ref[...], g_ref[...])
    hb = h.astype(BF16)
    scale = HEAD_DIM ** -0.5

    def proj(col):
        return jnp.dot(hb, w_ref[:, col * width:(col + 1) * width], preferred_element_type=F32)

    q_ref[:, :width] = (proj(0) * scale).astype(BF16)
    q_ref[:, width:] = (proj(1) * scale).astype(BF16)
    fk = proj(2)
    fk_ref[...] = fk
    k_ref[:, :width] = fk.astype(BF16)
    dk = proj(3)
    dk_ref[...] = dk
    k_ref[:, width:] = dk.astype(BF16)
    fv = proj(4)
    fv_ref[...] = fv
    v_ref[:, :width] = fv.astype(BF16)
    dv = proj(5)
    dv_ref[...] = dv
    v_ref[:, width:] = dv.astype(BF16)
    n_gate = gate_ref.shape[1] // width
    for j in range(n_gate):
        z = proj(6 + j) + bg_ref[:, j * width:(j + 1) * width]
        gate_ref[:, j * width:(j + 1) * width] = (1.0 / (1.0 + jnp.exp(-z))).astype(BF16)

    zf = jnp.dot(h, wf_ref[...], precision=HI, preferred_element_type=F32) + bf_ref[...]
    lf = jnp.minimum(zf, 0.0) - jnp.log1p(jnp.exp(-jnp.abs(zf)))
    lf_ref[...] = lf

    @pl.when(i % blocks_per_seq == 0)
    def _():
        carry_ref[...] = jnp.zeros_like(carry_ref)

    row = lax.broadcasted_iota(I32, (tm, tm), 0)
    col = lax.broadcasted_iota(I32, (tm, tm), 1)
    tri = jnp.where(col <= row, 1.0, 0.0).astype(BF16)
    c = _tri_dot(tri, lf) + carry_ref[...]
    c_ref[...] = c
    carry_ref[...] = c[tm - 1:tm, :]


def _proj(x, g, w_main, w_f, b_f, b_g, *, tm, blocks_per_seq):
    m, d = x.shape
    width = w_main.shape[1] // 10
    n_f = w_f.shape[1]
    gate_w = b_g.shape[1]
    row_spec = lambda n: pl.BlockSpec((tm, n), lambda i: (i, 0))
    out_shape = (
        jax.ShapeDtypeStruct((m, 2 * width), BF16),
        jax.ShapeDtypeStruct((m, 2 * width), BF16),
        jax.ShapeDtypeStruct((m, 2 * width), BF16),
        jax.ShapeDtypeStruct((m, width), F32),
        jax.ShapeDtypeStruct((m, width), F32),
        jax.ShapeDtypeStruct((m, width), F32),
        jax.ShapeDtypeStruct((m, width), F32),
        jax.ShapeDtypeStruct((m, n_f), F32),
        jax.ShapeDtypeStruct((m, n_f), F32),
        jax.ShapeDtypeStruct((m, gate_w), BF16),
    )
    out_specs = (row_spec(2 * width), row_spec(2 * width), row_spec(2 * width),
                 row_spec(width), row_spec(width), row_spec(width), row_spec(width),
                 row_spec(n_f), row_spec(n_f), row_spec(gate_w))
    return pl.pallas_call(
        functools.partial(_proj_kernel, blocks_per_seq=blocks_per_seq, width=width),
        grid=(m // tm,),
        in_specs=[row_spec(d), _const_spec((1, d)), _const_spec(w_main.shape), _const_spec(w_f.shape),
                  _const_spec((1, n_f)), _const_spec((1, gate_w))],
        out_specs=out_specs,
        out_shape=out_shape,
        scratch_shapes=[pltpu.VMEM((1, n_f), F32)],
        compiler_params=_cparams(1),
        name="proj",
    )(x, g, w_main, w_f, b_f, b_g)


def _attn_kernel(lam_ref, q_ref, qb_ref, k_ref, kb_ref, v_ref, gs_ref, o_ref,
                 acc_ref, m_ref, l_ref, *, blk, n_fox_units, out_scale):
    u = pl.program_id(1)
    qi = pl.program_id(2)
    q2 = jnp.concatenate([q_ref[...], qb_ref[...]], axis=1)
    lane2 = lax.broadcasted_iota(I32, q2.shape, 1)
    first_half = (lane2 & (LANES - 1)) < HEAD_DIM
    zero = jnp.zeros_like(q2)
    q_maps = (jnp.where(first_half, q2, zero), jnp.where(first_half, zero, q2))

    m_ref[...] = jnp.full_like(m_ref, NEG_INF)
    l_ref[...] = jnp.zeros_like(l_ref)
    acc_ref[...] = jnp.zeros_like(acc_ref)

    def step(kj, masked):
        off = pl.multiple_of(kj * blk, blk)
        k2 = jnp.concatenate([k_ref[pl.ds(off, blk), :], kb_ref[pl.ds(off, blk), :]], axis=1)
        v = v_ref[pl.ds(off, blk), :]
        for idx in range(2):
            s = lax.dot_general(q_maps[idx], k2, (((1,), (1,)), ((), ())),
                                preferred_element_type=F32)
            if masked:
                row = lax.broadcasted_iota(I32, s.shape, 0)
                col = lax.broadcasted_iota(I32, s.shape, 1)
                s = jnp.where(col <= row, s, NEG_INF)
            m_old = m_ref[idx]
            m_new = jnp.maximum(m_old, jnp.max(s, axis=1, keepdims=True))
            alpha = jnp.exp(m_old - m_new)
            p = jnp.exp(s - m_new)
            l_ref[idx] = alpha * l_ref[idx] + jnp.sum(p, axis=1, keepdims=True)
            acc_ref[idx] = acc_ref[idx] * alpha + jnp.dot(p.astype(BF16), v,
                                                          preferred_element_type=F32)
            m_ref[idx] = m_new

    def body(kj, carry):
        step(kj, False)
        return carry

    lax.fori_loop(0, qi, body, 0)
    step(qi, True)

    o_a = acc_ref[0] / l_ref[0]
    o_b = acc_ref[1] / l_ref[1]

    @pl.when(u < n_fox_units)
    def _():
        lane = lax.broadcasted_iota(I32, o_a.shape, 1)
        o_ref[...] = jnp.where(lane < HEAD_DIM, o_a, o_b).astype(BF16)

    @pl.when(u >= n_fox_units)
    def _():
        o = o_a - lam_ref[0] * o_b
        o_ref[...] = (_rms(o, gs_ref[...]) * out_scale).astype(BF16)


def _prompt_attention(lam, q, qb, k, kb, v, g_subln, *, blk, n_fox_units, out_scale):
    b, t, w = q.shape
    n_units = w // LANES
    q_spec = pl.BlockSpec((None, blk, LANES), lambda bi, u, qi: (bi, qi, u))
    kv_spec = pl.BlockSpec((None, t, LANES), lambda bi, u, qi: (bi, 0, u))
    return pl.pallas_call(
        functools.partial(_attn_kernel, blk=blk, n_fox_units=n_fox_units, out_scale=out_scale),
        grid=(b, n_units, t // blk),
        in_specs=[pl.BlockSpec(memory_space=pltpu.SMEM), q_spec, q_spec, kv_spec, kv_spec, kv_spec,
                  pl.BlockSpec((1, LANES), lambda bi, u, qi: (0, 0))],
        out_specs=q_spec,
        out_shape=jax.ShapeDtypeStruct((b, t, w), BF16),
        scratch_shapes=[pltpu.VMEM((2, blk, LANES), F32), pltpu.VMEM((2, blk, 1), F32),
                        pltpu.VMEM((2, blk, 1), F32)],
        compiler_params=_cparams(3),
        name="prompt_attn",
    )(lam, q, qb, k, kb, v, g_subln)


def _expand_rows(rows, e_ref):
    n = rows.shape[0]
    hi = _trunc_bf16(rows)
    mid = _trunc_bf16(rows - hi)
    lo = rows - hi - mid
    pad = jnp.zeros((8 - 3 * n, rows.shape[1]), F32)
    lhs = jnp.concatenate([hi, mid, lo, pad], axis=0).astype(BF16)
    r = jnp.dot(lhs, e_ref[...], preferred_element_type=F32)
    return r[0:n] + r[n:2 * n] + r[2 * n:3 * n]


def _decode_kernel(pt_ref, lam_ref, *refs, pages_per_step, page, past, n_fox, out_scale):
    del pt_ref
    pp = pages_per_step
    fk_refs = refs[0:pp]
    fv_refs = refs[pp:2 * pp]
    lf_refs = refs[2 * pp:3 * pp]
    dk_refs = refs[3 * pp:4 * pp]
    dv_refs = refs[4 * pp:5 * pp]
    (qf_ref, qd_ref, kfn_ref, vfn_ref, lfn_ref, kdn_ref, vdn_ref, slope_ref, ef_ref, ed_ref, gs_ref,
     of_ref, od_ref, mf_ref, lf_acc_ref, accf_ref, cpre_ref, md_ref, ld_ref, accd_ref) = refs[5 * pp:]
    g = pl.program_id(1)
    n_steps = pl.num_programs(1)

    @pl.when(g == 0)
    def _():
        mf_ref[...] = jnp.full_like(mf_ref, NEG_INF)
        md_ref[...] = jnp.full_like(md_ref, NEG_INF)
        lf_acc_ref[...] = jnp.zeros_like(lf_acc_ref)
        ld_ref[...] = jnp.zeros_like(ld_ref)
        accf_ref[...] = jnp.zeros_like(accf_ref)
        accd_ref[...] = jnp.zeros_like(accd_ref)
        cpre_ref[...] = jnp.zeros_like(cpre_ref)

    row = lax.broadcasted_iota(I32, (page, page), 0)
    col = lax.broadcasted_iota(I32, (page, page), 1)
    tri = jnp.where(col <= row, 1.0, 0.0).astype(BF16)
    pad_w = LANES - n_fox

    def tile_heads(v):
        blocks = []
        for hh in range(v.shape[1] // LANES):
            blk = v[:, hh * LANES:(hh + 1) * LANES]
            blocks += [blk, blk]
        return jnp.concatenate(blocks, axis=1)

    def update(s, vals, m_ref, l_ref, acc_ref, e_ref):
        m_old = m_ref[...]
        m_new = jnp.maximum(m_old, jnp.max(s, axis=0, keepdims=True))
        alpha = jnp.exp(m_old - m_new)
        p = jnp.exp(s - m_new)
        l_ref[...] = alpha * l_ref[...] + jnp.sum(p, axis=0, keepdims=True)
        m_ref[...] = m_new
        p_wide = jnp.dot(p.astype(BF16), e_ref[...], preferred_element_type=F32)
        pv = jnp.sum(p_wide * vals, axis=0, keepdims=True)
        acc_ref[...] = acc_ref[...] * _expand_rows(alpha, e_ref) + pv

    for j in range(pp):
        page_idx = g * pp + j
        s = jnp.dot(fk_refs[j][...].astype(BF16), qf_ref[...], preferred_element_type=F32)
        cs = _tri_dot(tri, lf_refs[j][...]) + cpre_ref[...]
        cpre_ref[...] = cs[page - 1:page, :]
        s = s - jnp.concatenate([cs, jnp.zeros((page, pad_w), F32)], axis=1)
        update(s, fv_refs[j][...], mf_ref, lf_acc_ref, accf_ref, ef_ref)
        sd = jnp.dot(dk_refs[j][...].astype(BF16), qd_ref[...], preferred_element_type=F32)
        pos = page_idx * page + lax.broadcasted_iota(I32, (page, 1), 0)
        dist = (past - pos).astype(F32)
        sd = sd - slope_ref[...] * dist
        update(sd, tile_heads(dv_refs[j][...]), md_ref, ld_ref, accd_ref, ed_ref)

    @pl.when(g == n_steps - 1)
    def _():
        s_new = jnp.dot(kfn_ref[...].astype(BF16), qf_ref[...], preferred_element_type=F32)
        c_tot = cpre_ref[...] + lfn_ref[...]
        s_new = s_new - jnp.concatenate([c_tot, jnp.zeros((1, pad_w), F32)], axis=1)
        update(s_new, vfn_ref[...], mf_ref, lf_acc_ref, accf_ref, ef_ref)
        sd_new = jnp.dot(kdn_ref[...].astype(BF16), qd_ref[...], preferred_element_type=F32)
        update(sd_new, tile_heads(vdn_ref[...]), md_ref, ld_ref, accd_ref, ed_ref)

        of_ref[...] = (accf_ref[...] * _expand_rows(1.0 / lf_acc_ref[...], ef_ref)).astype(BF16)
        od = accd_ref[...] * _expand_rows(1.0 / ld_ref[...], ed_ref)
        outs = []
        for hh in range(od.shape[1] // (2 * LANES)):
            o1 = od[:, (2 * hh) * LANES:(2 * hh + 1) * LANES]
            o2 = od[:, (2 * hh + 1) * LANES:(2 * hh + 2) * LANES]
            outs.append(_rms(o1 - lam_ref[0] * o2, gs_ref[...]) * out_scale)
        od_ref[...] = jnp.concatenate(outs, axis=1).astype(BF16)


def _decode_attention(page_table, lam, caches, qf_bd, qd_bd, new_rows, slope_row, e_f, e_d, g_subln,
                      *, layer, pages_per_step, out_scale):
    cfk, cfv, clf, cdk, cdv = caches
    kfn, vfn, lfn, kdn, vdn = new_rows
    n_seq, n_pages = page_table.shape
    n_phys, page = cfk.shape[1], cfk.shape[2]
    width = kfn.shape[-1]
    n_fox = lfn.shape[-1]
    pp = pages_per_step
    flat = lambda c: c.reshape((c.shape[0] * n_phys, page, -1))

    def page_spec(last, j):
        def imap(s, g, pt):
            return (layer * n_phys + pt[s * n_pages + g * pp + j], 0, 0)
        return pl.BlockSpec((None, page, last), imap)

    seq_spec = lambda r, c: pl.BlockSpec((None, r, c), lambda s, g, pt: (s, 0, 0))
    const2 = lambda r, c: pl.BlockSpec((r, c), lambda s, g, pt: (0, 0))
    in_specs = [pl.BlockSpec(memory_space=pltpu.SMEM)]
    args = [lam]
    for cache, last in ((cfk, width), (cfv, width), (clf, n_fox), (cdk, width), (cdv, width)):
        for j in range(pp):
            in_specs.append(page_spec(last, j))
            args.append(flat(cache))
    in_specs += [seq_spec(width, LANES), seq_spec(width, LANES),
                 seq_spec(1, width), seq_spec(1, width), seq_spec(1, n_fox),
                 seq_spec(1, width), seq_spec(1, width),
                 const2(1, LANES), const2(LANES, width), const2(LANES, 2 * width), const2(1, LANES)]
    args += [qf_bd, qd_bd, kfn, vfn, lfn, kdn, vdn, slope_row, e_f, e_d, g_subln]
    grid_spec = pltpu.PrefetchScalarGridSpec(
        num_scalar_prefetch=1,
        grid=(n_seq, n_pages // pp),
        in_specs=in_specs,
        out_specs=(seq_spec(1, width), seq_spec(1, width)),
        scratch_shapes=[pltpu.VMEM((1, LANES), F32), pltpu.VMEM((1, LANES), F32),
                        pltpu.VMEM((1, width), F32), pltpu.VMEM((1, n_fox), F32),
                        pltpu.VMEM((1, LANES), F32), pltpu.VMEM((1, LANES), F32),
                        pltpu.VMEM((1, 2 * width), F32)],
    )
    return pl.pallas_call(
        functools.partial(_decode_kernel, pages_per_step=pp, page=page, past=n_pages * page,
                          n_fox=n_fox, out_scale=out_scale),
        grid_spec=grid_spec,
        out_shape=(jax.ShapeDtypeStruct((n_seq, 1, width), BF16),
                   jax.ShapeDtypeStruct((n_seq, 1, width), BF16)),
        compiler_params=_cparams(2),
        name="decode_attn",
    )(page_table.reshape(-1), *args)


def _mix_kernel(o_ref, gate_ref, x_ref, wuf_ref, wud_ref, wo_ref, gn_ref, wr_ref, br_ref, base_ref,
                x1_ref, h2_ref, idx_ref, wgt_ref, rank_ref, cnt_ref, cnt_acc, *, n_experts):
    i = pl.program_id(0)
    tm = x_ref.shape[0]
    half = o_ref.shape[1] // 2
    d = x_ref.shape[1]
    up_f = jnp.dot(o_ref[:, :half], wuf_ref[...], preferred_element_type=F32)
    up_d = jnp.dot(o_ref[:, half:], wud_ref[...], preferred_element_type=F32)
    merged = gate_ref[:, :d].astype(F32) * up_f + gate_ref[:, d:].astype(F32) * up_d
    x1 = x_ref[...] + jnp.dot(merged.astype(BF16), wo_ref[...], preferred_element_type=F32)
    x1_ref[...] = x1
    h2 = _rms(x1, gn_ref[...])
    h2_ref[...] = h2

    logits = jnp.dot(h2, wr_ref[...], precision=HI, preferred_element_type=F32) + br_ref[...]
    lane = lax.broadcasted_iota(I32, logits.shape, 1).astype(F32)
    out_lane = lax.broadcasted_iota(I32, (tm, TOP_K), 1)
    remaining = logits
    picks, vals = [], []
    for _ in range(TOP_K):
        mx = jnp.max(remaining, axis=1, keepdims=True)
        sel = jnp.min(jnp.where(remaining == mx, lane, float(n_experts)), axis=1, keepdims=True)
        hit = lane == sel
        picks.append(hit)
        vals.append(mx)
        remaining = jnp.where(hit, NEG_INF, remaining)
    exps = [jnp.exp(v - vals[0]) for v in vals]
    denom = exps[0] + exps[1] + exps[2] + exps[3]

    @pl.when(i == 0)
    def _():
        cnt_acc[...] = base_ref[...]

    chosen = jnp.zeros(logits.shape, F32)
    for hit in picks:
        chosen = chosen + jnp.where(hit, 1.0, 0.0)
    row = lax.broadcasted_iota(I32, (tm, tm), 0)
    col = lax.broadcasted_iota(I32, (tm, tm), 1)
    strict = jnp.where(col < row, 1.0, 0.0).astype(BF16)
    before = jnp.dot(strict, chosen.astype(BF16), preferred_element_type=F32) + cnt_acc[...]
    cnt_acc[...] = cnt_acc[...] + jnp.sum(chosen, axis=0, keepdims=True)
    cnt_ref[...] = cnt_acc[...]

    idx_out = jnp.zeros((tm, TOP_K), I32)
    wgt_out = jnp.zeros((tm, TOP_K), F32)
    rank_out = jnp.zeros((tm, TOP_K), I32)
    for kk in range(TOP_K):
        sel = jnp.sum(jnp.where(picks[kk], lane, 0.0), axis=1, keepdims=True).astype(I32)
        rk = jnp.sum(jnp.where(picks[kk], before, 0.0), axis=1, keepdims=True).astype(I32)
        idx_out = jnp.where(out_lane == kk, sel, idx_out)
        wgt_out = jnp.where(out_lane == kk, exps[kk] / denom, wgt_out)
        rank_out = jnp.where(out_lane == kk, rk, rank_out)
    idx_ref[...] = idx_out
    wgt_ref[...] = wgt_out
    rank_ref[...] = rank_out


def _mix(o, gates, x, wuf, wud, wo, g_norm, w_router, b_router, base, *, tm):
    m, d = x.shape
    n_exp = w_router.shape[1]
    row_spec = lambda n: pl.BlockSpec((tm, n), lambda i: (i, 0))
    return pl.pallas_call(
        functools.partial(_mix_kernel, n_experts=n_exp),
        grid=(m // tm,),
        in_specs=[row_spec(o.shape[1]), row_spec(gates.shape[1]), row_spec(d),
                  _const_spec(wuf.shape), _const_spec(wud.shape), _const_spec(wo.shape),
                  _const_spec((1, d)), _const_spec(w_router.shape), _const_spec((1, n_exp)),
                  _const_spec((1, n_exp))],
        out_specs=(row_spec(d), row_spec(d), row_spec(TOP_K), row_spec(TOP_K), row_spec(TOP_K),
                   pl.BlockSpec((1, n_exp), lambda i: (0, 0))),
        out_shape=(jax.ShapeDtypeStruct((m, d), F32), jax.ShapeDtypeStruct((m, d), F32),
                   jax.ShapeDtypeStruct((m, TOP_K), I32), jax.ShapeDtypeStruct((m, TOP_K), F32),
                   jax.ShapeDtypeStruct((m, TOP_K), I32), jax.ShapeDtypeStruct((1, n_exp), F32)),
        scratch_shapes=[pltpu.VMEM((1, n_exp), F32)],
        compiler_params=_cparams(1),
        name="mix_router",
    )(o, gates, x, wuf, wud, wo, g_norm, w_router, b_router, base)


def _row_copy(src, src_row, dst, dst_row, sem):
    return pltpu.make_async_copy(src.at[pl.ds(src_row, 1), :], dst.at[pl.ds(dst_row, 1), :], sem)


def _dispatch_kernel(dest_ref, h_ref, rows_in_ref, rows_ref, sem, *, tc):
    del rows_in_ref
    base = pl.program_id(0) * tc

    def issue(t, carry):
        for kk in range(TOP_K):
            _row_copy(h_ref, base + t, rows_ref, dest_ref[t * TOP_K + kk], sem).start()
        return carry

    lax.fori_loop(0, tc, issue, 0)
    pltpu.make_async_copy(h_ref.at[pl.ds(0, tc * TOP_K), :], rows_ref.at[pl.ds(0, tc * TOP_K), :],
                          sem).wait()


def _dispatch(dest_flat, h, rows, *, tc):
    m, d = h.shape
    return pl.pallas_call(
        functools.partial(_dispatch_kernel, tc=tc),
        grid=(m // tc,),
        in_specs=[pl.BlockSpec((tc * TOP_K,), lambda i: (i,), memory_space=pltpu.SMEM),
                  pl.BlockSpec(memory_space=pl.ANY), pl.BlockSpec(memory_space=pl.ANY)],
        out_specs=pl.BlockSpec(memory_space=pl.ANY),
        out_shape=jax.ShapeDtypeStruct(rows.shape, rows.dtype),
        scratch_shapes=[pltpu.SemaphoreType.DMA(())],
        input_output_aliases={2: 0},
        compiler_params=_cparams(1, has_side_effects=True),
        name="dispatch",
    )(dest_flat, h, rows)


def _combine_kernel(dest_ref, y_ref, w_ref, x1_ref, g_ref, o_ref, buf, sem, *, tc, final_norm):
    def issue(t, carry):
        for kk in range(TOP_K):
            _row_copy(y_ref, dest_ref[t * TOP_K + kk], buf.at[kk], t, sem).start()
        return carry

    lax.fori_loop(0, tc, issue, 0)
    for kk in range(TOP_K):
        pltpu.make_async_copy(y_ref.at[pl.ds(0, tc), :], buf.at[kk], sem).wait()
    out = x1_ref[...]
    for kk in range(TOP_K):
        out = out + buf[kk] * w_ref[:, kk:kk + 1]
    if final_norm:
        out = _rms(out, g_ref[...])
    o_ref[...] = out


def _combine(dest_flat, y_rows, wgt, x1, g, *, tc, final_norm):
    m, d = x1.shape
    return pl.pallas_call(
        functools.partial(_combine_kernel, tc=tc, final_norm=final_norm),
        grid=(m // tc,),
        in_specs=[pl.BlockSpec((tc * TOP_K,), lambda i: (i,), memory_space=pltpu.SMEM),
                  pl.BlockSpec(memory_space=pl.ANY),
                  pl.BlockSpec((tc, TOP_K), lambda i: (i, 0)),
                  pl.BlockSpec((tc, d), lambda i: (i, 0)),
                  pl.BlockSpec((1, d), lambda i: (0, 0))],
        out_specs=pl.BlockSpec((tc, d), lambda i: (i, 0)),
        out_shape=jax.ShapeDtypeStruct((m, d), F32),
        scratch_shapes=[pltpu.VMEM((TOP_K, tc, d), F32), pltpu.SemaphoreType.DMA(())],
        compiler_params=_cparams(1),
        name="combine",
    )(dest_flat, y_rows, wgt, x1, g)


def _expert_kernel(be_ref, na_ref, x_ref, wgu_ref, bgu_ref, wd_ref, bd_ref, y_ref):
    del be_ref
    i = pl.program_id(0)
    d_ff = wd_ref.shape[0]

    @pl.when(i < na_ref[0])
    def _():
        gu = jnp.dot(x_ref[...].astype(BF16), wgu_ref[...], preferred_element_type=F32) + bgu_ref[...]
        glu = jnp.minimum(gu[:, :d_ff], SWIGLU_LIMIT)
        lin = jnp.clip(gu[:, d_ff:], -SWIGLU_LIMIT, SWIGLU_LIMIT)
        act = glu * (1.0 / (1.0 + jnp.exp(-SWIGLU_ALPHA * glu))) * (lin + 1.0)
        y_ref[...] = jnp.dot(act.astype(BF16), wd_ref[...], preferred_element_type=F32) + bd_ref[...]

    @pl.when(i >= na_ref[0])
    def _():
        y_ref[...] = jnp.zeros_like(y_ref)


def _experts(block_expert, n_active, x_rows, wgu, bgu, wd, bd):
    n_rows, d = x_rows.shape
    n_blocks = n_rows // ROW_BLOCK
    grid_spec = pltpu.PrefetchScalarGridSpec(
        num_scalar_prefetch=2,
        grid=(n_blocks,),
        in_specs=[pl.BlockSpec((ROW_BLOCK, d), lambda i, be, na: (i, 0)),
                  pl.BlockSpec((None,) + wgu.shape[1:], lambda i, be, na: (be[i], 0, 0)),
                  pl.BlockSpec((None,) + bgu.shape[1:], lambda i, be, na: (be[i], 0, 0)),
                  pl.BlockSpec((None,) + wd.shape[1:], lambda i, be, na: (be[i], 0, 0)),
                  pl.BlockSpec((None,) + bd.shape[1:], lambda i, be, na: (be[i], 0, 0))],
        out_specs=pl.BlockSpec((ROW_BLOCK, d), lambda i, be, na: (i, 0)),
    )
    return pl.pallas_call(
        _expert_kernel,
        grid_spec=grid_spec,
        out_shape=jax.ShapeDtypeStruct((n_rows, d), F32),
        compiler_params=_cparams(1),
        name="experts",
    )(block_expert, n_active, x_rows, wgu, bgu, wd, bd)


def _bias_lanes(cols, width):
    lead = cols[0].shape
    z = jnp.zeros(lead + (HEAD_DIM - len(cols),), F32)
    stacked = jnp.concatenate([c[..., None] for c in cols] + [z], axis=-1)
    return stacked.reshape(lead[:-1] + (width,))


def _attention_bias(c, slopes, t):
    m = c.shape[0]
    width = c.shape[1] * HEAD_DIM
    hi, mid, lo = _split3_xla(c)
    one = jnp.ones_like(c)
    qb_fox = _bias_lanes([hi, mid, lo, one, one, one], width)
    kb_fox = _bias_lanes([one, one, one, -hi, -mid, -lo], width)
    pos = jnp.arange(m, dtype=I32) % t
    p_hi = ((pos // LANES) * LANES).astype(F32)[:, None]
    p_lo = (pos % LANES).astype(F32)[:, None]
    sl = jnp.repeat(slopes, 2)[None, :]
    one_d = jnp.ones((m, sl.shape[1]), F32)
    qb_diff = _bias_lanes([-sl * p_hi, -sl * p_lo, one_d, one_d], width)
    kb_diff = _bias_lanes([one_d, one_d, sl * p_hi, sl * p_lo], width)
    qb = jnp.concatenate([qb_fox, qb_diff], axis=-1).astype(BF16)
    kb = jnp.concatenate([kb_fox, kb_diff], axis=-1).astype(BF16)
    return qb, kb


def _block_diag(q, n_cols):
    n, w = q.shape
    head = jnp.arange(w, dtype=I32) // HEAD_DIM
    sel = head[:, None] == jnp.arange(LANES, dtype=I32)[None, :]
    return jnp.where(sel[None], q[:, :, None], jnp.zeros((), q.dtype))


def _expansion(n_cols, block):
    r = jnp.arange(LANES, dtype=I32)[:, None]
    c = jnp.arange(n_cols * block, dtype=I32)[None, :] // block
    return (r == c).astype(BF16)


def kernel(x_prompt, x_sample, cache_fox_k, cache_fox_v, cache_fox_logf, cache_diff_k, cache_diff_v,
           page_table, g_attn_norm, w_in, b_forget, b_gate, lambda_q1, lambda_k1, lambda_q2, lambda_k2,
           g_subln, w_up_fox, w_up_diff, w_out, g_ffn_norm, w_router, b_router, w_gate_up, b_gate_up,
           w_down, b_down, g_final):
    depth = w_in.shape[0]
    b, t, d = x_prompt.shape
    n_seq = x_sample.shape[0]
    assert x_sample.shape[1] == 1
    n_fox = cache_fox_k.shape[3]
    n_diff = cache_diff_k.shape[3]
    width = n_fox * HEAD_DIM
    assert cache_diff_k.shape[3] * 2 * HEAD_DIM == width and cache_diff_v.shape[4] == 2 * HEAD_DIM
    n_exp = w_router.shape[2]
    m_p = b * t
    slopes = 2.0 ** (-8.0 * jnp.arange(1, n_diff + 1, dtype=F32) / n_diff)
    slope_row = jnp.zeros((1, LANES), F32).at[0, :2 * n_diff].set(jnp.repeat(slopes, 2))
    e_f = _expansion(n_fox, HEAD_DIM)
    e_d = _expansion(2 * n_diff, 2 * HEAD_DIM)
    attn_blk = min(512, t)
    tm_p = min(512, m_p)

    xp = x_prompt.reshape(m_p, d)
    xs = x_sample.reshape(n_seq, d)
    outs_p, outs_s = [], []
    for l in range(depth):
        lam_init = 0.8 - 0.6 * math.exp(-0.3 * l)
        lam = (jnp.exp(jnp.sum(lambda_q1[l].astype(F32) * lambda_k1[l].astype(F32)))
               - jnp.exp(jnp.sum(lambda_q2[l].astype(F32) * lambda_k2[l].astype(F32)))
               + lam_init).reshape(1)
        out_scale = 1.0 - lam_init
        w = w_in[l]
        o_fq, o_fk, o_fv, o_zf = 0, width, 2 * width, 3 * width
        o_dq = o_zf + n_fox
        o_dk, o_dv, o_zg = o_dq + width, o_dq + 2 * width, o_dq + 3 * width
        sl = lambda o, n=width: w[:, o:o + n]
        w_main = jnp.concatenate([sl(o_fq), sl(o_dq), sl(o_fk), sl(o_dk), sl(o_fv), sl(o_dv),
                                  w[:, o_zg:]], axis=1).astype(BF16)
        w_f = sl(o_zf, n_fox)
        g_attn = g_attn_norm[l][None]
        b_f = b_forget[l][None]
        b_g = b_gate[l][None]
        gs = g_subln[l][None]
        wuf, wud, wo = (w_up_fox[l].astype(BF16), w_up_diff[l].astype(BF16), w_out[l].astype(BF16))
        g_ffn = g_ffn_norm[l][None]
        last = l == depth - 1
        g_out = g_final[None] if last else jnp.ones((1, d), F32)

        (q_p, k_p, v_p, fk_p, fv_p, dk_p, dv_p, lf_p, c_p, gate_p) = _proj(
            xp, g_attn, w_main, w_f, b_f, b_g, tm=tm_p, blocks_per_seq=t // tm_p)
        qb_p, kb_p = _attention_bias(c_p, slopes, t)
        r3 = lambda a: a.reshape(b, t, a.shape[-1])
        o_p = _prompt_attention(lam, r3(q_p), r3(qb_p), r3(k_p), r3(kb_p), r3(v_p), gs,
                                blk=attn_blk, n_fox_units=n_fox // 2, out_scale=out_scale)
        x1_p, h2_p, idx_p, wgt_p, rank_p, cnt_p = _mix(
            o_p.reshape(m_p, 2 * width), gate_p, xp, wuf, wud, wo, g_ffn, w_router[l], b_router[l][None],
            jnp.zeros((1, n_exp), F32), tm=tm_p)

        (q_s, k_s, v_s, fk_s, fv_s, dk_s, dv_s, lf_s, _, gate_s) = _proj(
            xs, g_attn, w_main, w_f, b_f, b_g, tm=n_seq, blocks_per_seq=1)
        of_s, od_s = _decode_attention(
            page_table, lam,
            (cache_fox_k, cache_fox_v, cache_fox_logf, cache_diff_k, cache_diff_v),
            _block_diag(q_s[:, :width], n_fox), _block_diag(q_s[:, width:], 2 * n_diff),
            (fk_s[:, None], fv_s[:, None], lf_s[:, None], dk_s[:, None], dv_s[:, None]),
            slope_row, e_f, e_d, gs, layer=l, pages_per_step=4, out_scale=out_scale)
        o_s = jnp.concatenate([of_s[:, 0], od_s[:, 0]], axis=1)
        x1_s, h2_s, idx_s, wgt_s, rank_s, cnt_all = _mix(
            o_s, gate_s, xs, wuf, wud, wo, g_ffn, w_router[l], b_router[l][None], cnt_p, tm=n_seq)

        counts = cnt_all[0].astype(I32)
        padded = (counts + ROW_BLOCK - 1) // ROW_BLOCK * ROW_BLOCK
        pad_end = jnp.cumsum(padded)
        pad_start = pad_end - padded
        n_assign = (m_p + n_seq) * TOP_K
        n_blocks = -(-n_assign // ROW_BLOCK) + n_exp
        block_expert = jnp.minimum(
            jnp.searchsorted(pad_end, jnp.arange(n_blocks, dtype=I32) * ROW_BLOCK, side="right"),
            n_exp - 1).astype(I32)
        n_active = (pad_end[-1:] // ROW_BLOCK).astype(I32)
        dest_p = (pad_start[idx_p] + rank_p).reshape(-1)
        dest_s = (pad_start[idx_s] + rank_s).reshape(-1)

        x_rows = jnp.zeros((n_blocks * ROW_BLOCK, d), F32)
        x_rows = _dispatch(dest_p, h2_p, x_rows, tc=ROW_BLOCK)
        x_rows = _dispatch(dest_s, h2_s, x_rows, tc=ROW_BLOCK)
        y_rows = _experts(block_expert, n_active, x_rows, w_gate_up[l].astype(BF16), b_gate_up[l][:, None],
                          w_down[l].astype(BF16), b_down[l][:, None])
        xp = _combine(dest_p, y_rows, wgt_p, x1_p, g_out, tc=ROW_BLOCK, final_norm=last)
        xs = _combine(dest_s, y_rows, wgt_s, x1_s, g_out, tc=ROW_BLOCK, final_norm=last)

        outs_p.append((fk_p.reshape(b, t, n_fox, HEAD_DIM), fv_p.reshape(b, t, n_fox, HEAD_DIM),
                       lf_p.reshape(b, t, n_fox), dk_p.reshape(b, t, n_diff, 2, HEAD_DIM),
                       dv_p.reshape(b, t, n_diff, 2 * HEAD_DIM)))
        outs_s.append((fk_s.reshape(n_seq, 1, n_fox, HEAD_DIM), fv_s.reshape(n_seq, 1, n_fox, HEAD_DIM),
                       lf_s.reshape(n_seq, 1, n_fox), dk_s.reshape(n_seq, 1, n_diff, 2, HEAD_DIM),
                       dv_s.reshape(n_seq, 1, n_diff, 2 * HEAD_DIM)))

    stack = lambda rows: [jnp.stack(r) for r in zip(*rows)]
    return (xp.reshape(b, t, d), xs.reshape(n_seq, 1, d), *stack(outs_p), *stack(outs_s))
```

```python
import functools
import math

import numpy as np
import jax
import jax.numpy as jnp
from jax import lax
from jax.experimental import pallas as pl
from jax.experimental.pallas import tpu as pltpu

F32 = jnp.float32
BF16 = jnp.bfloat16
I32 = jnp.int32

HEAD_DIM = 64
LANES = 128
EPS = 1e-5
SWIGLU_LIMIT = 7.0
SWIGLU_ALPHA = 1.702
TOP_K = 4
ROW_BLOCK = 256
ROUTE_BLOCK = 128
TOKEN_BLOCK = 512
VMEM_LIMIT = 56 * 1024 * 1024
NEG_INF = float("-inf")
LOG2E = math.log2(math.e)
HI = lax.Precision.HIGHEST
NT_DIMS = (((1,), (1,)), ((), ()))


def _cparams(n_axes, **kw):
    return pltpu.CompilerParams(dimension_semantics=("arbitrary",) * n_axes,
                                vmem_limit_bytes=VMEM_LIMIT, **kw)


def _const_spec(shape):
    nd = len(shape)
    return pl.BlockSpec(shape, lambda *_: (0,) * nd, pipeline_mode=pl.Buffered(1))


def _trunc_bf16(x):
    bits = pltpu.bitcast(x, jnp.uint32) & jnp.uint32(0xFFFF0000)
    return pltpu.bitcast(bits, F32)


def _split3(x):
    hi = _trunc_bf16(x)
    r = x - hi
    mid = _trunc_bf16(r)
    return hi, mid, r - mid


def _rms(x, g):
    ms = jnp.mean(x * x, axis=-1, keepdims=True)
    return x * lax.rsqrt(ms + EPS) * g


def _proj_kernel(x_ref, g_ref, w_ref, wt_ref, wf_ref, wft_ref, bf_ref, bft_ref, bg_ref, slope_ref,
                 pq_ref, pk_ref,
                 q_ref, qb_ref, k_ref, kb_ref, vt_ref, fkt_ref, fvt_ref, dkt_ref, dv_ref, lft_ref, gate_ref,
                 carry_ref, *, blocks_per_seq, width):
    i = pl.program_id(0)
    tm = x_ref.shape[0]
    h = _rms(x_ref[...], g_ref[...])
    hb = h.astype(BF16)
    scale = HEAD_DIM ** -0.5 * LOG2E

    def proj(col):
        return jnp.dot(hb, w_ref[:, col * width:(col + 1) * width], preferred_element_type=F32)

    def proj_t(grp):
        return lax.dot_general(wt_ref[grp * width:(grp + 1) * width, :], hb, NT_DIMS,
                               preferred_element_type=F32)

    q_ref[:, :width] = (proj(0) * scale).astype(BF16)
    q_ref[:, width:] = (proj(1) * scale).astype(BF16)
    k_ref[:, :width] = proj(2).astype(BF16)
    k_ref[:, width:] = proj(3).astype(BF16)
    dv_ref[...] = proj(4)
    fkt_ref[...] = proj_t(0)
    dkt_ref[...] = proj_t(1)
    fvt = proj_t(2)
    fvt_ref[...] = fvt
    vt_ref[:width, :] = fvt.astype(BF16)
    vt_ref[width:, :] = proj_t(3).astype(BF16)
    n_gate = gate_ref.shape[1] // width
    for j in range(n_gate):
        z = proj(5 + j) + bg_ref[:, j * width:(j + 1) * width]
        gate_ref[:, j * width:(j + 1) * width] = (1.0 / (1.0 + jnp.exp(-z))).astype(BF16)

    def log_sigmoid(z):
        return jnp.minimum(z, 0.0) - jnp.log1p(jnp.exp(-jnp.abs(z)))

    zft = lax.dot_general(wft_ref[...], h, NT_DIMS, precision=HI, preferred_element_type=F32)
    lft_ref[...] = log_sigmoid(zft + bft_ref[...])
    zf = jnp.dot(h, wf_ref[...], precision=HI, preferred_element_type=F32)
    lf = log_sigmoid(zf + bf_ref[...])

    @pl.when(i % blocks_per_seq == 0)
    def _():
        carry_ref[...] = jnp.zeros_like(carry_ref)

    row = lax.broadcasted_iota(I32, (tm, tm), 0)
    col = lax.broadcasted_iota(I32, (tm, tm), 1)
    tri = jnp.where(col <= row, 1.0, 0.0).astype(BF16)
    c = carry_ref[...]
    for part in _split3(lf):
        c = c + jnp.dot(tri, part.astype(BF16), preferred_element_type=F32)
    carry_ref[...] = c[tm - 1:tm, :]

    pos = ((i % blocks_per_seq) * tm + lax.broadcasted_iota(I32, (tm, 1), 0)).astype(F32)
    a = jnp.concatenate([c * LOG2E, -(slope_ref[...] * pos)], axis=1)
    hi, mid, lo = _split3(a)
    pieces = jnp.concatenate([hi, mid, lo, jnp.ones_like(a)], axis=1).astype(BF16)
    qb_ref[...] = jnp.dot(pieces, pq_ref[...], preferred_element_type=F32).astype(BF16)
    kb_ref[...] = jnp.dot(pieces, pk_ref[...], preferred_element_type=F32).astype(BF16)


def _proj(x, g, w_nn, w_nt, w_f, b_f, b_g, slope_l2, pq, pk, *, tm, n_batch):
    m, d = x.shape
    width = w_nt.shape[0] // 4
    n_f = w_f.shape[1]
    gate_w = b_g.shape[1]
    t = m // n_batch
    bps = t // tm
    row_spec = lambda n: pl.BlockSpec((tm, n), lambda i: (i, 0))
    tr_spec = lambda r: pl.BlockSpec((None, r, tm), lambda i: (i // bps, 0, i % bps))
    out_shape = (
        jax.ShapeDtypeStruct((m, 2 * width), BF16),
        jax.ShapeDtypeStruct((m, 2 * width), BF16),
        jax.ShapeDtypeStruct((m, 2 * width), BF16),
        jax.ShapeDtypeStruct((m, 2 * width), BF16),
        jax.ShapeDtypeStruct((m // tm, 2 * width, tm), BF16),
        jax.ShapeDtypeStruct((n_batch, width, t), F32),
        jax.ShapeDtypeStruct((n_batch, width, t), F32),
        jax.ShapeDtypeStruct((n_batch, width, t), F32),
        jax.ShapeDtypeStruct((m, width), F32),
        jax.ShapeDtypeStruct((n_batch, n_f, t), F32),
        jax.ShapeDtypeStruct((m, gate_w), BF16),
    )
    out_specs = (row_spec(2 * width), row_spec(2 * width), row_spec(2 * width), row_spec(2 * width),
                 pl.BlockSpec((None, 2 * width, tm), lambda i: (i, 0, 0)),
                 tr_spec(width), tr_spec(width), tr_spec(width), row_spec(width), tr_spec(n_f),
                 row_spec(gate_w))
    consts = (g, w_nn, w_nt, w_f, w_f.T, b_f, b_f.T, b_g, slope_l2, pq, pk)
    return pl.pallas_call(
        functools.partial(_proj_kernel, blocks_per_seq=bps, width=width),
        grid=(m // tm,),
        in_specs=[row_spec(d)] + [_const_spec(c.shape) for c in consts],
        out_specs=out_specs,
        out_shape=out_shape,
        scratch_shapes=[pltpu.VMEM((1, n_f), F32)],
        compiler_params=_cparams(1),
        name="proj",
    )(x, *consts)


def _bias_placement(n_cols):
    pq = np.zeros((4 * n_cols, n_cols * HEAD_DIM), np.float32)
    pk = np.zeros_like(pq)
    for j in range(n_cols):
        base = j * HEAD_DIM
        for part in range(3):
            pq[part * n_cols + j, base + part] = 1.0
            pq[3 * n_cols + j, base + 3 + part] = 1.0
            pk[3 * n_cols + j, base + part] = 1.0
            pk[part * n_cols + j, base + 3 + part] = -1.0
    return jnp.asarray(pq, BF16), jnp.asarray(pk, BF16)


def _attn_kernel(lam_ref, q_ref, qb_ref, k_ref, kb_ref, vt_ref, gs_ref, o_ref,
                 acc_ref, m_ref, s0_ref, s1_ref, *, blk, n_fox_units, out_scale):
    u = pl.program_id(1)
    qi = pl.program_id(2)
    q2 = jnp.concatenate([q_ref[...], qb_ref[...]], axis=1)
    lane2 = lax.broadcasted_iota(I32, q2.shape, 1)
    first_half = (lane2 & (LANES - 1)) < HEAD_DIM
    zero = jnp.zeros_like(q2)
    q_both = jnp.concatenate([jnp.where(first_half, q2, zero), jnp.where(first_half, zero, q2)], axis=0)
    ones_rows = jnp.ones((16, blk), BF16)

    m_ref[...] = jnp.full_like(m_ref, NEG_INF)
    acc_ref[...] = jnp.zeros_like(acc_ref)

    def scores(kj, st_ref):
        off = pl.multiple_of(kj * blk, blk)
        k2 = jnp.concatenate([k_ref[pl.ds(off, blk), :], kb_ref[pl.ds(off, blk), :]], axis=1)
        st_ref[...] = lax.dot_general(k2, q_both, NT_DIMS, preferred_element_type=F32)

    def accumulate(kj, st_ref, masked):
        vt = jnp.concatenate([vt_ref[kj], ones_rows], axis=0)
        st = st_ref[...]
        if masked:
            krow = lax.broadcasted_iota(I32, st.shape, 0)
            qcol = lax.broadcasted_iota(I32, st.shape, 1) & (blk - 1)
            st = jnp.where(krow <= qcol, st, NEG_INF)
        m_old = m_ref[...]
        m_new = jnp.maximum(m_old, jnp.max(st, axis=0, keepdims=True))
        alpha = jnp.exp2(m_old - m_new)
        p = jnp.exp2((st - m_new).astype(BF16))
        acc_ref[...] = acc_ref[...] * alpha + jnp.dot(vt, p, preferred_element_type=F32)
        m_ref[...] = m_new

    scores(0, s0_ref)

    def pair(jj, carry):
        scores(2 * jj + 1, s1_ref)
        accumulate(2 * jj, s0_ref, False)
        scores(2 * jj + 2, s0_ref)
        accumulate(2 * jj + 1, s1_ref, False)
        return carry

    lax.fori_loop(0, qi // 2, pair, 0)

    @pl.when(qi % 2 == 0)
    def _():
        accumulate(qi, s0_ref, True)

    @pl.when(qi % 2 == 1)
    def _():
        scores(qi, s1_ref)
        accumulate(qi - 1, s0_ref, False)
        accumulate(qi, s1_ref, True)

    acc = acc_ref[...]
    o_a = acc[:LANES, :blk] / acc[LANES:LANES + 1, :blk]
    o_b = acc[:LANES, blk:] / acc[LANES:LANES + 1, blk:]

    @pl.when(u < n_fox_units)
    def _():
        r = lax.broadcasted_iota(I32, o_a.shape, 0)
        o_ref[...] = jnp.where(r < HEAD_DIM, o_a, o_b).T.astype(BF16)

    @pl.when(u >= n_fox_units)
    def _():
        o = (o_a - lam_ref[0] * o_b).T
        o_ref[...] = (_rms(o, gs_ref[...]) * out_scale).astype(BF16)


def _prompt_attention(lam, q, qb, k, kb, vt, g_subln, *, blk, n_fox_units, out_scale):
    b, t, w = q.shape
    n_units = w // LANES
    nkb = t // blk
    q_spec = pl.BlockSpec((None, blk, LANES), lambda bi, u, qi: (bi, qi, u))
    k_spec = pl.BlockSpec((None, t, LANES), lambda bi, u, qi: (bi, 0, u))
    vt_spec = pl.BlockSpec((nkb, LANES, blk), lambda bi, u, qi: (bi, u, 0))
    return pl.pallas_call(
        functools.partial(_attn_kernel, blk=blk, n_fox_units=n_fox_units, out_scale=out_scale),
        grid=(b, n_units, nkb),
        in_specs=[pl.BlockSpec(memory_space=pltpu.SMEM), q_spec, q_spec, k_spec, k_spec, vt_spec,
                  pl.BlockSpec((1, LANES), lambda bi, u, qi: (0, 0))],
        out_specs=q_spec,
        out_shape=jax.ShapeDtypeStruct((b, t, w), BF16),
        scratch_shapes=[pltpu.VMEM((LANES + 16, 2 * blk), F32), pltpu.VMEM((1, 2 * blk), F32),
                        pltpu.VMEM((blk, 2 * blk), F32), pltpu.VMEM((blk, 2 * blk), F32)],
        compiler_params=_cparams(3),
        name="prompt_attn",
    )(lam, q, qb, k, kb, vt, g_subln)


def _decode_kernel(pt_ref, lam_ref, kf_hbm, vf_hbm, lf_hbm, kd_hbm, vd_hbm,
                   qft_ref, qdt_ref, kfn_ref, vfn_ref, lfn_ref, kdn_ref, vdn_ref, slope_ref, tri_ref, gs_ref,
                   oft_ref, od_ref,
                   kf_buf, vf_buf, lf_buf, kd_buf, vd_buf, sem,
                   qf_col, qd_col, mf_ref, sf_ref, accf_ref, cpre_ref, md_ref, sd_ref, accd_ref,
                   *, pp, page, past, out_scale):
    s_idx = pl.program_id(0)
    c = pl.program_id(1)
    n_chunks = pl.num_programs(1)
    n = s_idx * n_chunks + c
    total = pl.num_programs(0) * n_chunks
    slot = n % 2
    n_cols = mf_ref.shape[0]
    rows = kf_buf.shape[2]
    n_dh = accd_ref.shape[0]
    lane = lax.broadcasted_iota(I32, (1, LANES), 1)
    seq_lane = lane == s_idx % LANES

    def chunk_copies(chunk, slot_):
        cps = []
        for j in range(pp):
            pg = pt_ref[chunk * pp + j]
            for hbm, buf in ((kf_hbm, kf_buf), (vf_hbm, vf_buf), (lf_hbm, lf_buf), (kd_hbm, kd_buf),
                             (vd_hbm, vd_buf)):
                cps.append(pltpu.make_async_copy(hbm.at[pg], buf.at[slot_, j], sem.at[slot_]))
        return cps

    @pl.when(n == 0)
    def _():
        for cp in chunk_copies(0, 0):
            cp.start()

    @pl.when(n + 1 < total)
    def _():
        for cp in chunk_copies(n + 1, 1 - slot):
            cp.start()

    def pick(ref):
        col = jnp.sum(jnp.where(seq_lane, ref[...], 0.0), axis=1, keepdims=True)
        return jnp.broadcast_to(col, (ref.shape[0], LANES))

    @pl.when(c == 0)
    def _():
        qf_col[...] = pick(qft_ref)
        qd_col[...] = pick(qdt_ref)
        mf_ref[...] = jnp.full_like(mf_ref, NEG_INF)
        md_ref[...] = jnp.full_like(md_ref, NEG_INF)
        sf_ref[...] = jnp.zeros_like(sf_ref)
        sd_ref[...] = jnp.zeros_like(sd_ref)
        accf_ref[...] = jnp.zeros_like(accf_ref)
        accd_ref[...] = jnp.zeros_like(accd_ref)
        cpre_ref[...] = jnp.zeros_like(cpre_ref)

    @pl.when(jnp.logical_and(c == 0, s_idx % LANES == 0))
    def _():
        oft_ref[...] = jnp.zeros_like(oft_ref)

    def scores(kt, qcol_ref):
        out = []
        for hh in range(n_cols):
            sl = slice(hh * HEAD_DIM, (hh + 1) * HEAD_DIM)
            out.append(jnp.sum(kt[sl, :] * qcol_ref[sl, :], axis=0, keepdims=True))
        return jnp.concatenate(out, axis=0)

    def softmax_step(s, m_ref, s_ref):
        m_old = m_ref[...]
        m_new = jnp.maximum(m_old, jnp.max(s, axis=1, keepdims=True))
        alpha = jnp.exp2(m_old - m_new)
        p = jnp.exp2(s - m_new)
        s_ref[...] = alpha * s_ref[...] + jnp.sum(p, axis=1, keepdims=True)
        m_ref[...] = m_new
        return alpha, p

    def fox_update(s, vt):
        alpha, p = softmax_step(s, mf_ref, sf_ref)
        for hh in range(n_cols):
            sl = slice(hh * HEAD_DIM, (hh + 1) * HEAD_DIM)
            accf_ref[sl, :] = accf_ref[sl, :] * alpha[hh:hh + 1, :] + vt[sl, :] * p[hh:hh + 1, :]

    first_lane = lane == 0

    for cp in chunk_copies(n, slot):
        cp.wait()

    for j in range(pp):
        page_idx = c * pp + j
        s = scores(kf_buf[slot, j], qf_col)
        lf = lf_buf[slot, j]
        hi, mid, lo = _split3(lf)
        parts = jnp.concatenate([hi, mid, lo, jnp.zeros_like(lf)], axis=0).astype(BF16)
        r = jnp.dot(parts, tri_ref[...], preferred_element_type=F32)
        cs = r[0:n_cols] + r[n_cols:2 * n_cols] + r[2 * n_cols:3 * n_cols] + cpre_ref[...]
        cpre_ref[...] = cpre_ref[...] + jnp.sum(lf, axis=1, keepdims=True)
        fox_update(s - cs * LOG2E, vf_buf[slot, j])
        sd = scores(kd_buf[slot, j], qd_col)
        dist = (past - (page_idx * page + lane)).astype(F32)
        alpha, p = softmax_step(sd - slope_ref[...] * dist, md_ref, sd_ref)
        pb = p.astype(BF16)
        for hh in range(n_dh):
            vh = vd_buf[slot, j, :, hh, :].astype(BF16)
            accd_ref[hh] = accd_ref[hh] * alpha + jnp.dot(pb, vh, preferred_element_type=F32)

    @pl.when(c == n_chunks - 1)
    def _():
        s_new = scores(pick(kfn_ref), qf_col)
        c_tot = cpre_ref[...] + pick(lfn_ref)
        s_new = jnp.where(first_lane, s_new - c_tot * LOG2E, NEG_INF)
        fox_update(s_new, pick(vfn_ref))
        sd_new = jnp.where(first_lane, scores(pick(kdn_ref), qd_col), NEG_INF)
        alpha, p = softmax_step(sd_new, md_ref, sd_ref)
        p_new = jnp.sum(p, axis=1, keepdims=True)
        for hh in range(n_dh):
            accd_ref[hh] = accd_ref[hh] * alpha + p_new * vdn_ref[hh:hh + 1, :]

        inv_f = 1.0 / sf_ref[...]
        for hh in range(n_cols):
            sl = slice(hh * HEAD_DIM, (hh + 1) * HEAD_DIM)
            col = jnp.sum(accf_ref[sl, :], axis=1, keepdims=True) * inv_f[hh:hh + 1, :]
            oft_ref[sl, :] = jnp.where(seq_lane, jnp.broadcast_to(col, (HEAD_DIM, LANES)), oft_ref[sl, :])
        inv_d = 1.0 / sd_ref[...]
        for hh in range(n_dh):
            a = accd_ref[hh] * inv_d
            o = a[2 * hh:2 * hh + 1, :] - lam_ref[0] * a[2 * hh + 1:2 * hh + 2, :]
            od_ref[hh:hh + 1, :] = _rms(o, gs_ref[...]) * out_scale


def _decode_attention(page_table, lam, caches, qft, qdt, new_cols, vdn, slope_col, tri, g_subln,
                      *, pages_per_step, out_scale):
    kf_c, vf_c, lf_c, kd_c, vd_c = caches
    kfn, vfn, lfn, kdn = new_cols
    n_seq, n_pages = page_table.shape
    rows, page = kf_c.shape[1], kf_c.shape[2]
    n_cols = lf_c.shape[1]
    n_dh = vd_c.shape[2]
    assert n_seq % LANES == 0 and page == LANES and n_pages % pages_per_step == 0
    pp = pages_per_step
    any_spec = pl.BlockSpec(memory_space=pl.ANY)
    col_spec = lambda r: pl.BlockSpec((r, LANES), lambda s, c, pt: (0, s // LANES))
    const2 = lambda r, c_: pl.BlockSpec((r, c_), lambda s, c, pt: (0, 0))
    grid_spec = pltpu.PrefetchScalarGridSpec(
        num_scalar_prefetch=1,
        grid=(n_seq, n_pages // pp),
        in_specs=[pl.BlockSpec(memory_space=pltpu.SMEM), any_spec, any_spec, any_spec, any_spec, any_spec,
                  col_spec(rows), col_spec(rows), col_spec(rows), col_spec(rows), col_spec(n_cols),
                  col_spec(rows), pl.BlockSpec((None, n_dh, LANES), lambda s, c, pt: (s, 0, 0)),
                  const2(n_cols, LANES), const2(page, page), const2(1, LANES)],
        out_specs=(col_spec(rows), pl.BlockSpec((None, n_dh, LANES), lambda s, c, pt: (s, 0, 0))),
        scratch_shapes=[pltpu.VMEM((2, pp, rows, page), F32), pltpu.VMEM((2, pp, rows, page), F32),
                        pltpu.VMEM((2, pp, n_cols, page), F32), pltpu.VMEM((2, pp, rows, page), F32),
                        pltpu.VMEM((2, pp, page, n_dh, LANES), F32), pltpu.SemaphoreType.DMA((2,)),
                        pltpu.VMEM((rows, LANES), F32), pltpu.VMEM((rows, LANES), F32),
                        pltpu.VMEM((n_cols, 1), F32), pltpu.VMEM((n_cols, 1), F32),
                        pltpu.VMEM((rows, LANES), F32), pltpu.VMEM((n_cols, 1), F32),
                        pltpu.VMEM((n_cols, 1), F32), pltpu.VMEM((n_cols, 1), F32),
                        pltpu.VMEM((n_dh, n_cols, LANES), F32)],
    )
    return pl.pallas_call(
        functools.partial(_decode_kernel, pp=pp, page=page, past=n_pages * page, out_scale=out_scale),
        grid_spec=grid_spec,
        out_shape=(jax.ShapeDtypeStruct((rows, n_seq), F32),
                   jax.ShapeDtypeStruct((n_seq, n_dh, LANES), F32)),
        compiler_params=_cparams(2),
        name="decode_attn",
    )(page_table.reshape(-1), lam, kf_c, vf_c, lf_c, kd_c, vd_c,
      qft, qdt, kfn, vfn, lfn, kdn, vdn, slope_col, tri, g_subln)


def _mix_kernel(o_ref, gate_ref, x_ref, wuf_ref, wud_ref, wo_ref, gn_ref, wr_ref, br_ref, base_ref,
                x1_ref, h2_ref, idx_ref, wgt_ref, rank_ref, cnt_ref, cnt_acc, *, n_experts):
    i = pl.program_id(0)
    tm = x_ref.shape[0]
    half = o_ref.shape[1] // 2
    d = x_ref.shape[1]
    up_f = jnp.dot(o_ref[:, :half], wuf_ref[...], preferred_element_type=F32)
    up_d = jnp.dot(o_ref[:, half:], wud_ref[...], preferred_element_type=F32)
    merged = gate_ref[:, :d].astype(F32) * up_f + gate_ref[:, d:].astype(F32) * up_d
    x1 = x_ref[...] + jnp.dot(merged.astype(BF16), wo_ref[...], preferred_element_type=F32)
    x1_ref[...] = x1
    h2 = _rms(x1, gn_ref[...])
    h2_ref[...] = h2

    logits = jnp.dot(h2, wr_ref[...], precision=HI, preferred_element_type=F32) + br_ref[...]
    lane = lax.broadcasted_iota(I32, logits.shape, 1).astype(F32)
    out_lane = lax.broadcasted_iota(I32, (tm, TOP_K), 1)
    remaining = logits
    picks, vals = [], []
    for _ in range(TOP_K):
        mx = jnp.max(remaining, axis=1, keepdims=True)
        sel = jnp.min(jnp.where(remaining == mx, lane, float(n_experts)), axis=1, keepdims=True)
        hit = lane == sel
        picks.append(hit)
        vals.append(mx)
        remaining = jnp.where(hit, NEG_INF, remaining)
    exps = [jnp.exp(v - vals[0]) for v in vals]
    denom = exps[0] + exps[1] + exps[2] + exps[3]

    @pl.when(i == 0)
    def _():
        cnt_acc[...] = base_ref[...]

    chosen = jnp.zeros(logits.shape, F32)
    for hit in picks:
        chosen = chosen + jnp.where(hit, 1.0, 0.0)
    row = lax.broadcasted_iota(I32, (tm, tm), 0)
    col = lax.broadcasted_iota(I32, (tm, tm), 1)
    strict = jnp.where(col < row, 1.0, 0.0).astype(BF16)
    before = jnp.dot(strict, chosen.astype(BF16), preferred_element_type=F32) + cnt_acc[...]
    cnt_acc[...] = cnt_acc[...] + jnp.sum(chosen, axis=0, keepdims=True)
    cnt_ref[...] = cnt_acc[...]

    idx_out = jnp.zeros((tm, TOP_K), I32)
    wgt_out = jnp.zeros((tm, TOP_K), F32)
    rank_out = jnp.zeros((tm, TOP_K), I32)
    for kk in range(TOP_K):
        sel = jnp.sum(jnp.where(picks[kk], lane, 0.0), axis=1, keepdims=True).astype(I32)
        rk = jnp.sum(jnp.where(picks[kk], before, 0.0), axis=1, keepdims=True).astype(I32)
        idx_out = jnp.where(out_lane == kk, sel, idx_out)
        wgt_out = jnp.where(out_lane == kk, exps[kk] / denom, wgt_out)
        rank_out = jnp.where(out_lane == kk, rk, rank_out)
    idx_ref[...] = idx_out
    wgt_ref[...] = wgt_out
    rank_ref[...] = rank_out


def _mix(o, gates, x, wuf, wud, wo, g_norm, w_router, b_router, base, *, tm):
    m, d = x.shape
    n_exp = w_router.shape[1]
    row_spec = lambda n: pl.BlockSpec((tm, n), lambda i: (i, 0))
    return pl.pallas_call(
        functools.partial(_mix_kernel, n_experts=n_exp),
        grid=(m // tm,),
        in_specs=[row_spec(o.shape[1]), row_spec(gates.shape[1]), row_spec(d),
                  _const_spec(wuf.shape), _const_spec(wud.shape), _const_spec(wo.shape),
                  _const_spec((1, d)), _const_spec(w_router.shape), _const_spec((1, n_exp)),
                  _const_spec((1, n_exp))],
        out_specs=(row_spec(d), row_spec(d), row_spec(TOP_K), row_spec(TOP_K), row_spec(TOP_K),
                   pl.BlockSpec((1, n_exp), lambda i: (0, 0))),
        out_shape=(jax.ShapeDtypeStruct((m, d), F32), jax.ShapeDtypeStruct((m, d), F32),
                   jax.ShapeDtypeStruct((m, TOP_K), I32), jax.ShapeDtypeStruct((m, TOP_K), F32),
                   jax.ShapeDtypeStruct((m, TOP_K), I32), jax.ShapeDtypeStruct((1, n_exp), F32)),
        scratch_shapes=[pltpu.VMEM((1, n_exp), F32)],
        compiler_params=_cparams(1),
        name="mix_router",
    )(o, gates, x, wuf, wud, wo, g_norm, w_router, b_router, base)


def _row_copy(src, src_row, dst, dst_row, sem):
    return pltpu.make_async_copy(src.at[pl.ds(src_row, 1), :], dst.at[pl.ds(dst_row, 1), :], sem)


def _dispatch_kernel(dest_ref, h_ref, rows_in_ref, rows_ref, sem, *, tc):
    del rows_in_ref

    def issue(t, carry):
        for kk in range(TOP_K):
            _row_copy(h_ref, t, rows_ref, dest_ref[t * TOP_K + kk], sem).start()
        return carry

    lax.fori_loop(0, tc, issue, 0)
    for kk in range(TOP_K):
        pltpu.make_async_copy(h_ref, rows_ref.at[pl.ds(0, tc), :], sem).wait()


def _dispatch(dest_flat, h, rows, *, tc):
    m, d = h.shape
    return pl.pallas_call(
        functools.partial(_dispatch_kernel, tc=tc),
        grid=(m // tc,),
        in_specs=[pl.BlockSpec((tc * TOP_K,), lambda i: (i,), memory_space=pltpu.SMEM),
                  pl.BlockSpec((tc, d), lambda i: (i, 0)), pl.BlockSpec(memory_space=pl.ANY)],
        out_specs=pl.BlockSpec(memory_space=pl.ANY),
        out_shape=jax.ShapeDtypeStruct(rows.shape, rows.dtype),
        scratch_shapes=[pltpu.SemaphoreType.DMA(())],
        input_output_aliases={2: 0},
        compiler_params=_cparams(1, has_side_effects=True),
        name="dispatch",
    )(dest_flat, h, rows)


def _combine_kernel(dest_ref, y_ref, w_ref, x1_ref, g_ref, o_ref, buf, sem, *, tc, final_norm):
    def issue(t, carry):
        for kk in range(TOP_K):
            _row_copy(y_ref, dest_ref[t * TOP_K + kk], buf.at[kk], t, sem).start()
        return carry

    lax.fori_loop(0, tc, issue, 0)
    for kk in range(TOP_K):
        pltpu.make_async_copy(y_ref.at[pl.ds(0, tc), :], buf.at[kk], sem).wait()
    out = x1_ref[...]
    for kk in range(TOP_K):
        out = out + buf[kk] * w_ref[:, kk:kk + 1]
    if final_norm:
        out = _rms(out, g_ref[...])
    o_ref[...] = out


def _combine(dest_flat, y_rows, wgt, x1, g, *, tc, final_norm):
    m, d = x1.shape
    return pl.pallas_call(
        functools.partial(_combine_kernel, tc=tc, final_norm=final_norm),
        grid=(m // tc,),
        in_specs=[pl.BlockSpec((tc * TOP_K,), lambda i: (i,), memory_space=pltpu.SMEM),
                  pl.BlockSpec(memory_space=pl.ANY),
                  pl.BlockSpec((tc, TOP_K), lambda i: (i, 0)),
                  pl.BlockSpec((tc, d), lambda i: (i, 0)),
                  pl.BlockSpec((1, d), lambda i: (0, 0))],
        out_specs=pl.BlockSpec((tc, d), lambda i: (i, 0)),
        out_shape=jax.ShapeDtypeStruct((m, d), F32),
        scratch_shapes=[pltpu.VMEM((TOP_K, tc, d), F32), pltpu.SemaphoreType.DMA(())],
        compiler_params=_cparams(1),
        name="combine",
    )(dest_flat, y_rows, wgt, x1, g)


def _expert_kernel(be_ref, na_ref, x_ref, wgu_ref, bgu_ref, wd_ref, bd_ref, y_ref, wgu_bf, wd_bf):
    i = pl.program_id(0)
    d_ff = wd_ref.shape[0]
    active = i < na_ref[0]
    new_expert = jnp.logical_or(i == 0, be_ref[i] != be_ref[jnp.maximum(i - 1, 0)])

    @pl.when(jnp.logical_and(active, new_expert))
    def _():
        wgu_bf[...] = wgu_ref[...].astype(BF16)
        wd_bf[...] = wd_ref[...].astype(BF16)

    @pl.when(active)
    def _():
        gu = jnp.dot(x_ref[...].astype(BF16), wgu_bf[...], preferred_element_type=F32) + bgu_ref[...]
        glu = jnp.minimum(gu[:, :d_ff], SWIGLU_LIMIT)
        lin = jnp.clip(gu[:, d_ff:], -SWIGLU_LIMIT, SWIGLU_LIMIT)
        act = glu * (1.0 / (1.0 + jnp.exp(-SWIGLU_ALPHA * glu))) * (lin + 1.0)
        y_ref[...] = jnp.dot(act.astype(BF16), wd_bf[...], preferred_element_type=F32) + bd_ref[...]

    @pl.when(i >= na_ref[0])
    def _():
        y_ref[...] = jnp.zeros_like(y_ref)


def _experts(block_expert, n_active, x_rows, wgu, bgu, wd, bd):
    n_rows, d = x_rows.shape
    n_blocks = n_rows // ROW_BLOCK
    grid_spec = pltpu.PrefetchScalarGridSpec(
        num_scalar_prefetch=2,
        grid=(n_blocks,),
        in_specs=[pl.BlockSpec((ROW_BLOCK, d), lambda i, be, na: (i, 0)),
                  pl.BlockSpec((None,) + wgu.shape[1:], lambda i, be, na: (be[i], 0, 0)),
                  pl.BlockSpec((None,) + bgu.shape[1:], lambda i, be, na: (be[i], 0, 0)),
                  pl.BlockSpec((None,) + wd.shape[1:], lambda i, be, na: (be[i], 0, 0)),
                  pl.BlockSpec((None,) + bd.shape[1:], lambda i, be, na: (be[i], 0, 0))],
        out_specs=pl.BlockSpec((ROW_BLOCK, d), lambda i, be, na: (i, 0)),
        scratch_shapes=[pltpu.VMEM(wgu.shape[1:], BF16), pltpu.VMEM(wd.shape[1:], BF16)],
    )
    return pl.pallas_call(
        _expert_kernel,
        grid_spec=grid_spec,
        out_shape=jax.ShapeDtypeStruct((n_rows, d), F32),
        compiler_params=_cparams(1),
        name="experts",
    )(block_expert, n_active, x_rows, wgu, bgu, wd, bd)


def kernel(x_prompt, x_sample, cache_fox_k, cache_fox_v, cache_fox_logf, cache_diff_k, cache_diff_v,
           page_table, g_attn_norm, w_in, b_forget, b_gate, lambda_q1, lambda_k1, lambda_q2, lambda_k2,
           g_subln, w_up_fox, w_up_diff, w_out, g_ffn_norm, w_router, b_router, w_gate_up, b_gate_up,
           w_down, b_down, g_final):
    depth = w_in.shape[0]
    b, t, d = x_prompt.shape
    n_seq = x_sample.shape[0]
    assert x_sample.shape[1] == 1
    n_phys, page, n_fox = cache_fox_k.shape[1:4]
    n_diff = cache_diff_k.shape[3]
    width = n_fox * HEAD_DIM
    assert 2 * n_diff == n_fox and cache_diff_v.shape[4] == 2 * HEAD_DIM
    n_exp = w_router.shape[2]
    m_p = b * t
    slopes = 2.0 ** (-8.0 * jnp.arange(1, n_diff + 1, dtype=F32) / n_diff)
    slope_l2 = (jnp.repeat(slopes, 2) * LOG2E)
    slope_col = jnp.broadcast_to(slope_l2[:, None], (n_fox, LANES))
    pq, pk = _bias_placement(2 * n_fox)
    tri_u = (jnp.arange(page)[:, None] <= jnp.arange(page)[None, :]).astype(BF16)
    tm_p = min(TOKEN_BLOCK, t)

    xp = x_prompt.reshape(m_p, d)
    xs = x_sample.reshape(n_seq, d)
    outs_p, outs_s = [], []
    for l in range(depth):
        lam_init = 0.8 - 0.6 * math.exp(-0.3 * l)
        lam = (jnp.exp(jnp.sum(lambda_q1[l].astype(F32) * lambda_k1[l].astype(F32)))
               - jnp.exp(jnp.sum(lambda_q2[l].astype(F32) * lambda_k2[l].astype(F32)))
               + lam_init).reshape(1)
        out_scale = 1.0 - lam_init
        w = w_in[l]
        o_fq, o_fk, o_fv, o_zf = 0, width, 2 * width, 3 * width
        o_dq = o_zf + n_fox
        o_dk, o_dv, o_zg = o_dq + width, o_dq + 2 * width, o_dq + 3 * width
        sl = lambda o, n=width: w[:, o:o + n]
        w_nn = jnp.concatenate([sl(o_fq), sl(o_dq), sl(o_fk), sl(o_dk), sl(o_dv), w[:, o_zg:]],
                               axis=1).astype(BF16)
        w_nt = jnp.concatenate([sl(o_fk), sl(o_dk), sl(o_fv), sl(o_dv)], axis=1).T.astype(BF16)
        w_f = sl(o_zf, n_fox)
        g_attn = g_attn_norm[l][None]
        b_f = b_forget[l][None]
        b_g = b_gate[l][None]
        gs = g_subln[l][None]
        wuf, wud, wo = (w_up_fox[l].astype(BF16), w_up_diff[l].astype(BF16), w_out[l].astype(BF16))
        g_ffn = g_ffn_norm[l][None]
        last = l == depth - 1
        g_out = g_final[None] if last else jnp.ones((1, d), F32)
        proj = functools.partial(_proj, g=g_attn, w_nn=w_nn, w_nt=w_nt, w_f=w_f, b_f=b_f, b_g=b_g,
                                 slope_l2=slope_l2[None], pq=pq, pk=pk)

        (q_p, qb_p, k_p, kb_p, vt_p, fkt_p, fvt_p, dkt_p, dv_p, lft_p, gate_p) = proj(
            xp, tm=tm_p, n_batch=b)
        r3 = lambda a: a.reshape(b, t, a.shape[-1])
        o_p = _prompt_attention(lam, r3(q_p), r3(qb_p), r3(k_p), r3(kb_p), vt_p, gs,
                                blk=tm_p, n_fox_units=n_fox // 2, out_scale=out_scale)
        x1_p, h2_p, idx_p, wgt_p, rank_p, cnt_p = _mix(
            o_p.reshape(m_p, 2 * width), gate_p, xp, wuf, wud, wo, g_ffn, w_router[l], b_router[l][None],
            jnp.zeros((1, n_exp), F32), tm=tm_p)

        (q_s, _, _, _, _, fkt_s, fvt_s, dkt_s, dv_s, lft_s, gate_s) = proj(xs, tm=n_seq, n_batch=1)
        caches = (jnp.transpose(cache_fox_k[l], (0, 2, 3, 1)).reshape(n_phys, width, page),
                  jnp.transpose(cache_fox_v[l], (0, 2, 3, 1)).reshape(n_phys, width, page),
                  jnp.transpose(cache_fox_logf[l], (0, 2, 1)),
                  jnp.transpose(cache_diff_k[l], (0, 2, 3, 4, 1)).reshape(n_phys, width, page),
                  cache_diff_v[l])
        q_t = q_s.astype(F32).T
        oft, od = _decode_attention(
            page_table, lam, caches, q_t[:width], q_t[width:],
            (fkt_s[0], fvt_s[0], lft_s[0], dkt_s[0]), dv_s.reshape(n_seq, n_diff, 2 * HEAD_DIM),
            slope_col, tri_u, gs, pages_per_step=4, out_scale=out_scale)
        o_s = jnp.concatenate([oft.T, od.reshape(n_seq, width)], axis=1).astype(BF16)
        x1_s, h2_s, idx_s, wgt_s, rank_s, cnt_all = _mix(
            o_s, gate_s, xs, wuf, wud, wo, g_ffn, w_router[l], b_router[l][None], cnt_p, tm=n_seq)

        counts = cnt_all[0].astype(I32)
        padded = (counts + ROW_BLOCK - 1) // ROW_BLOCK * ROW_BLOCK
        pad_end = jnp.cumsum(padded)
        pad_start = pad_end - padded
        n_assign = (m_p + n_seq) * TOP_K
        n_blocks = -(-n_assign // ROW_BLOCK) + n_exp
        block_start = jnp.arange(n_blocks, dtype=I32) * ROW_BLOCK
        block_expert = jnp.minimum(jnp.sum((pad_end[None, :] <= block_start[:, None]).astype(I32), axis=1),
                                   n_exp - 1).astype(I32)
        n_active = (pad_end[-1:] // ROW_BLOCK).astype(I32)
        dest_p = (pad_start[idx_p] + rank_p).reshape(-1)
        dest_s = (pad_start[idx_s] + rank_s).reshape(-1)

        x_rows = jnp.zeros((n_blocks * ROW_BLOCK, d), F32)
        x_rows = _dispatch(dest_p, h2_p, x_rows, tc=2 * ROUTE_BLOCK)
        x_rows = _dispatch(dest_s, h2_s, x_rows, tc=ROUTE_BLOCK)
        y_rows = _experts(block_expert, n_active, x_rows, w_gate_up[l], b_gate_up[l][:, None],
                          w_down[l], b_down[l][:, None])
        xp = _combine(dest_p, y_rows, wgt_p, x1_p, g_out, tc=ROUTE_BLOCK, final_norm=last)
        xs = _combine(dest_s, y_rows, wgt_s, x1_s, g_out, tc=ROUTE_BLOCK, final_norm=last)

        def kv_rows(kt_, n_b, n_t):
            return jnp.transpose(kt_.reshape(n_b, -1, HEAD_DIM, n_t), (0, 3, 1, 2))

        def cache_rows(fkt_, fvt_, lft_, dkt_, dv_, n_b, n_t):
            return (kv_rows(fkt_, n_b, n_t), kv_rows(fvt_, n_b, n_t), jnp.transpose(lft_, (0, 2, 1)),
                    kv_rows(dkt_, n_b, n_t).reshape(n_b, n_t, n_diff, 2, HEAD_DIM),
                    dv_.reshape(n_b, n_t, n_diff, 2 * HEAD_DIM))

        outs_p.append(cache_rows(fkt_p, fvt_p, lft_p, dkt_p, dv_p, b, t))
        rows_s = cache_rows(fkt_s, fvt_s, lft_s, dkt_s, dv_s, 1, n_seq)
        outs_s.append(tuple(a.reshape((n_seq, 1) + a.shape[2:]) for a in rows_s))

    stack = lambda rows: [jnp.stack(r) for r in zip(*rows)]
    return (xp.reshape(b, t, d), xs.reshape(n_seq, 1, d), *stack(outs_p), *stack(outs_s))
```

```python
import functools
import math

import numpy as np
import jax
import jax.numpy as jnp
from jax import lax
from jax.experimental import pallas as pl
from jax.experimental.pallas import tpu as pltpu

F32 = jnp.float32
BF16 = jnp.bfloat16
I32 = jnp.int32

HEAD_DIM = 64
LANES = 128
EPS = 1e-5
SWIGLU_LIMIT = 7.0
SWIGLU_ALPHA = 1.702
TOP_K = 4
ROW_BLOCK = 256
ROUTE_BLOCK = 128
TOKEN_BLOCK = 512
VMEM_LIMIT = 56 * 1024 * 1024
NEG_INF = float("-inf")
LOG2E = math.log2(math.e)
HI = lax.Precision.HIGHEST
NT_DIMS = (((1,), (1,)), ((), ()))


def _cparams(n_axes, **kw):
    return pltpu.CompilerParams(dimension_semantics=("arbitrary",) * n_axes,
                                vmem_limit_bytes=VMEM_LIMIT, **kw)


def _const_spec(shape):
    nd = len(shape)
    return pl.BlockSpec(shape, lambda *_: (0,) * nd, pipeline_mode=pl.Buffered(1))


def _trunc_bf16(x):
    bits = pltpu.bitcast(x, jnp.uint32) & jnp.uint32(0xFFFF0000)
    return pltpu.bitcast(bits, F32)


def _split3(x):
    hi = _trunc_bf16(x)
    r = x - hi
    mid = _trunc_bf16(r)
    return hi, mid, r - mid


def _rms(x, g):
    ms = jnp.mean(x * x, axis=-1, keepdims=True)
    return x * lax.rsqrt(ms + EPS) * g


def _proj_kernel(x_ref, g_ref, w_ref, wt_ref, wf_ref, wft_ref, bf_ref, bft_ref, bg_ref, slope_ref,
                 pq_ref, pk_ref,
                 q_ref, qb_ref, k_ref, kb_ref, vt_ref, fkt_ref, fvt_ref, dkt_ref, dv_ref, lft_ref, gate_ref,
                 carry_ref, *, blocks_per_seq, width):
    i = pl.program_id(0)
    tm = x_ref.shape[0]
    h = _rms(x_ref[...], g_ref[...])
    hb = h.astype(BF16)
    scale = HEAD_DIM ** -0.5 * LOG2E

    def proj(col):
        return jnp.dot(hb, w_ref[:, col * width:(col + 1) * width], preferred_element_type=F32)

    def proj_t(grp):
        return lax.dot_general(wt_ref[grp * width:(grp + 1) * width, :], hb, NT_DIMS,
                               preferred_element_type=F32)

    q_ref[:, :width] = (proj(0) * scale).astype(BF16)
    q_ref[:, width:] = (proj(1) * scale).astype(BF16)
    k_ref[:, :width] = proj(2).astype(BF16)
    k_ref[:, width:] = proj(3).astype(BF16)
    dv_ref[...] = proj(4)
    fkt_ref[...] = proj_t(0)
    dkt_ref[...] = proj_t(1)
    fvt = proj_t(2)
    fvt_ref[...] = fvt
    vt_ref[:width, :] = fvt.astype(BF16)
    vt_ref[width:, :] = proj_t(3).astype(BF16)
    n_gate = gate_ref.shape[1] // width
    for j in range(n_gate):
        z = proj(5 + j) + bg_ref[:, j * width:(j + 1) * width]
        gate_ref[:, j * width:(j + 1) * width] = (1.0 / (1.0 + jnp.exp(-z))).astype(BF16)

    def log_sigmoid(z):
        return jnp.minimum(z, 0.0) - jnp.log1p(jnp.exp(-jnp.abs(z)))

    zft = lax.dot_general(wft_ref[...], h, NT_DIMS, precision=HI, preferred_element_type=F32)
    lft_ref[...] = log_sigmoid(zft + bft_ref[...])
    zf = jnp.dot(h, wf_ref[...], precision=HI, preferred_element_type=F32)
    lf = log_sigmoid(zf + bf_ref[...])

    @pl.when(i % blocks_per_seq == 0)
    def _():
        carry_ref[...] = jnp.zeros_like(carry_ref)

    row = lax.broadcasted_iota(I32, (tm, tm), 0)
    col = lax.broadcasted_iota(I32, (tm, tm), 1)
    tri = jnp.where(col <= row, 1.0, 0.0).astype(BF16)
    c = carry_ref[...]
    for part in _split3(lf):
        c = c + jnp.dot(tri, part.astype(BF16), preferred_element_type=F32)
    carry_ref[...] = c[tm - 1:tm, :]

    pos = ((i % blocks_per_seq) * tm + lax.broadcasted_iota(I32, (tm, 1), 0)).astype(F32)
    a = jnp.concatenate([c * LOG2E, -(slope_ref[...] * pos)], axis=1)
    hi, mid, lo = _split3(a)
    pieces = jnp.concatenate([hi, mid, lo, jnp.ones_like(a)], axis=1).astype(BF16)
    qb_ref[...] = jnp.dot(pieces, pq_ref[...], preferred_element_type=F32).astype(BF16)
    kb_ref[...] = jnp.dot(pieces, pk_ref[...], preferred_element_type=F32).astype(BF16)


def _proj(x, g, w_nn, w_nt, w_f, b_f, b_g, slope_l2, pq, pk, *, tm, n_batch):
    m, d = x.shape
    width = w_nt.shape[0] // 4
    n_f = w_f.shape[1]
    gate_w = b_g.shape[1]
    t = m // n_batch
    bps = t // tm
    row_spec = lambda n: pl.BlockSpec((tm, n), lambda i: (i, 0))
    tr_spec = lambda r: pl.BlockSpec((None, r, tm), lambda i: (i // bps, 0, i % bps))
    out_shape = (
        jax.ShapeDtypeStruct((m, 2 * width), BF16),
        jax.ShapeDtypeStruct((m, 2 * width), BF16),
        jax.ShapeDtypeStruct((m, 2 * width), BF16),
        jax.ShapeDtypeStruct((m, 2 * width), BF16),
        jax.ShapeDtypeStruct((m // tm, 2 * width, tm), BF16),
        jax.ShapeDtypeStruct((n_batch, width, t), F32),
        jax.ShapeDtypeStruct((n_batch, width, t), F32),
        jax.ShapeDtypeStruct((n_batch, width, t), F32),
        jax.ShapeDtypeStruct((m, width), F32),
        jax.ShapeDtypeStruct((n_batch, n_f, t), F32),
        jax.ShapeDtypeStruct((m, gate_w), BF16),
    )
    out_specs = (row_spec(2 * width), row_spec(2 * width), row_spec(2 * width), row_spec(2 * width),
                 pl.BlockSpec((None, 2 * width, tm), lambda i: (i, 0, 0)),
                 tr_spec(width), tr_spec(width), tr_spec(width), row_spec(width), tr_spec(n_f),
                 row_spec(gate_w))
    consts = (g, w_nn, w_nt, w_f, w_f.T, b_f, b_f.T, b_g, slope_l2, pq, pk)
    return pl.pallas_call(
        functools.partial(_proj_kernel, blocks_per_seq=bps, width=width),
        grid=(m // tm,),
        in_specs=[row_spec(d)] + [_const_spec(c.shape) for c in consts],
        out_specs=out_specs,
        out_shape=out_shape,
        scratch_shapes=[pltpu.VMEM((1, n_f), F32)],
        compiler_params=_cparams(1),
        name="proj",
    )(x, *consts)


def _bias_placement(n_cols):
    pq = np.zeros((4 * n_cols, n_cols * HEAD_DIM), np.float32)
    pk = np.zeros_like(pq)
    for j in range(n_cols):
        base = j * HEAD_DIM
        for part in range(3):
            pq[part * n_cols + j, base + part] = 1.0
            pq[3 * n_cols + j, base + 3 + part] = 1.0
            pk[3 * n_cols + j, base + part] = 1.0
            pk[part * n_cols + j, base + 3 + part] = -1.0
    return jnp.asarray(pq, BF16), jnp.asarray(pk, BF16)


def _attn_kernel(lam_ref, q_ref, qb_ref, k_ref, kb_ref, vt_ref, gs_ref, o_ref,
                 acc_ref, m_ref, s0_ref, s1_ref, *, blk, n_fox_units, out_scale):
    u = pl.program_id(1)
    qi = pl.program_id(2)
    q2 = jnp.concatenate([q_ref[...], qb_ref[...]], axis=1)
    lane2 = lax.broadcasted_iota(I32, q2.shape, 1)
    first_half = (lane2 & (LANES - 1)) < HEAD_DIM
    zero = jnp.zeros_like(q2)
    q_both = jnp.concatenate([jnp.where(first_half, q2, zero), jnp.where(first_half, zero, q2)], axis=0)
    ones_rows = jnp.ones((16, blk), BF16)

    m_ref[...] = jnp.full_like(m_ref, NEG_INF)
    acc_ref[...] = jnp.zeros_like(acc_ref)

    def scores(kj, st_ref):
        off = pl.multiple_of(kj * blk, blk)
        k2 = jnp.concatenate([k_ref[pl.ds(off, blk), :], kb_ref[pl.ds(off, blk), :]], axis=1)
        st_ref[...] = lax.dot_general(k2, q_both, NT_DIMS, preferred_element_type=F32)

    def accumulate(kj, st_ref, masked):
        vt = jnp.concatenate([vt_ref[kj], ones_rows], axis=0)
        st = st_ref[...]
        if masked:
            krow = lax.broadcasted_iota(I32, st.shape, 0)
            qcol = lax.broadcasted_iota(I32, st.shape, 1) & (blk - 1)
            st = jnp.where(krow <= qcol, st, NEG_INF)
        m_old = m_ref[...]
        m_new = jnp.maximum(m_old, jnp.max(st, axis=0, keepdims=True))
        alpha = jnp.exp2(m_old - m_new)
        p = jnp.exp2((st - m_new).astype(BF16))
        acc_ref[...] = acc_ref[...] * alpha + jnp.dot(vt, p, preferred_element_type=F32)
        m_ref[...] = m_new

    scores(0, s0_ref)

    def pair(jj, carry):
        scores(2 * jj + 1, s1_ref)
        accumulate(2 * jj, s0_ref, False)
        scores(2 * jj + 2, s0_ref)
        accumulate(2 * jj + 1, s1_ref, False)
        return carry

    lax.fori_loop(0, qi // 2, pair, 0)

    @pl.when(qi % 2 == 0)
    def _():
        accumulate(qi, s0_ref, True)

    @pl.when(qi % 2 == 1)
    def _():
        scores(qi, s1_ref)
        accumulate(qi - 1, s0_ref, False)
        accumulate(qi, s1_ref, True)

    acc = acc_ref[...]
    o_a = acc[:LANES, :blk] / acc[LANES:LANES + 1, :blk]
    o_b = acc[:LANES, blk:] / acc[LANES:LANES + 1, blk:]

    @pl.when(u < n_fox_units)
    def _():
        r = lax.broadcasted_iota(I32, o_a.shape, 0)
        o_ref[...] = jnp.where(r < HEAD_DIM, o_a, o_b).T.astype(BF16)

    @pl.when(u >= n_fox_units)
    def _():
        o = (o_a - lam_ref[0] * o_b).T
        o_ref[...] = (_rms(o, gs_ref[...]) * out_scale).astype(BF16)


def _prompt_attention(lam, q, qb, k, kb, vt, g_subln, *, blk, n_fox_units, out_scale):
    b, t, w = q.shape
    n_units = w // LANES
    nkb = t // blk
    q_spec = pl.BlockSpec((None, blk, LANES), lambda bi, u, qi: (bi, qi, u))
    k_spec = pl.BlockSpec((None, t, LANES), lambda bi, u, qi: (bi, 0, u))
    vt_spec = pl.BlockSpec((nkb, LANES, blk), lambda bi, u, qi: (bi, u, 0))
    return pl.pallas_call(
        functools.partial(_attn_kernel, blk=blk, n_fox_units=n_fox_units, out_scale=out_scale),
        grid=(b, n_units, nkb),
        in_specs=[pl.BlockSpec(memory_space=pltpu.SMEM), q_spec, q_spec, k_spec, k_spec, vt_spec,
                  pl.BlockSpec((1, LANES), lambda bi, u, qi: (0, 0))],
        out_specs=q_spec,
        out_shape=jax.ShapeDtypeStruct((b, t, w), BF16),
        scratch_shapes=[pltpu.VMEM((LANES + 16, 2 * blk), F32), pltpu.VMEM((1, 2 * blk), F32),
                        pltpu.VMEM((blk, 2 * blk), F32), pltpu.VMEM((blk, 2 * blk), F32)],
        compiler_params=_cparams(3),
        name="prompt_attn",
    )(lam, q, qb, k, kb, vt, g_subln)


def _decode_kernel(pt_ref, lam_ref, kf_hbm, vf_hbm, lf_hbm, kd_hbm, vd_hbm,
                   qft_ref, qdt_ref, kfn_ref, vfn_ref, lfn_ref, kdn_ref, vdn_ref, slope_ref, tri_ref, gs_ref,
                   oft_ref, od_ref,
                   kf_buf, vf_buf, lf_buf, kd_buf, vd_buf, sem,
                   qf_col, qd_col, mf_ref, sf_ref, accf_ref, cpre_ref, md_ref, sd_ref, accd_ref,
                   *, pp, page, past, out_scale):
    s_idx = pl.program_id(0)
    c = pl.program_id(1)
    n_chunks = pl.num_programs(1)
    n = s_idx * n_chunks + c
    total = pl.num_programs(0) * n_chunks
    slot = n % 2
    n_cols = mf_ref.shape[0]
    rows = kf_buf.shape[2]
    n_dh = accd_ref.shape[0]
    lane = lax.broadcasted_iota(I32, (1, LANES), 1)
    seq_lane = lane == s_idx % LANES

    def chunk_copies(chunk, slot_):
        cps = []
        for j in range(pp):
            pg = pt_ref[chunk * pp + j]
            for hbm, buf in ((kf_hbm, kf_buf), (vf_hbm, vf_buf), (lf_hbm, lf_buf), (kd_hbm, kd_buf),
                             (vd_hbm, vd_buf)):
                cps.append(pltpu.make_async_copy(hbm.at[pg], buf.at[slot_, j], sem.at[slot_]))
        return cps

    @pl.when(n == 0)
    def _():
        for cp in chunk_copies(0, 0):
            cp.start()

    @pl.when(n + 1 < total)
    def _():
        for cp in chunk_copies(n + 1, 1 - slot):
            cp.start()

    def pick(ref):
        col = jnp.sum(jnp.where(seq_lane, ref[...], 0.0), axis=1, keepdims=True)
        return jnp.broadcast_to(col, (ref.shape[0], LANES))

    @pl.when(c == 0)
    def _():
        qf_col[...] = pick(qft_ref)
        qd_col[...] = pick(qdt_ref)
        mf_ref[...] = jnp.full_like(mf_ref, NEG_INF)
        md_ref[...] = jnp.full_like(md_ref, NEG_INF)
        sf_ref[...] = jnp.zeros_like(sf_ref)
        sd_ref[...] = jnp.zeros_like(sd_ref)
        accf_ref[...] = jnp.zeros_like(accf_ref)
        accd_ref[...] = jnp.zeros_like(accd_ref)
        cpre_ref[...] = jnp.zeros_like(cpre_ref)

    @pl.when(jnp.logical_and(c == 0, s_idx % LANES == 0))
    def _():
        oft_ref[...] = jnp.zeros_like(oft_ref)

    def scores(kt, qcol_ref):
        out = []
        for hh in range(n_cols):
            sl = slice(hh * HEAD_DIM, (hh + 1) * HEAD_DIM)
            out.append(jnp.sum(kt[sl, :] * qcol_ref[sl, :], axis=0, keepdims=True))
        return jnp.concatenate(out, axis=0)

    def softmax_step(s, m_ref, s_ref):
        m_old = m_ref[...]
        m_new = jnp.maximum(m_old, jnp.max(s, axis=1, keepdims=True))
        alpha = jnp.exp2(m_old - m_new)
        p = jnp.exp2(s - m_new)
        s_ref[...] = alpha * s_ref[...] + jnp.sum(p, axis=1, keepdims=True)
        m_ref[...] = m_new
        return alpha, p

    def fox_update(s, vts):
        alpha, p = softmax_step(s, mf_ref, sf_ref)
        for hh in range(n_cols):
            sl = slice(hh * HEAD_DIM, (hh + 1) * HEAD_DIM)
            acc = accf_ref[sl, :] * alpha[hh:hh + 1, :]
            for j, vt in enumerate(vts):
                acc = acc + vt[sl, :] * p[hh:hh + 1, j * LANES:(j + 1) * LANES]
            accf_ref[sl, :] = acc

    first_lane = lane == 0

    for cp in chunk_copies(n, slot):
        cp.wait()

    s_pages = []
    cpre = cpre_ref[...]
    for j in range(pp):
        s = scores(kf_buf[slot, j], qf_col)
        lf = lf_buf[slot, j]
        hi, mid, lo = _split3(lf)
        parts = jnp.concatenate([hi, mid, lo, jnp.zeros_like(lf)], axis=0).astype(BF16)
        r = jnp.dot(parts, tri_ref[...], preferred_element_type=F32)
        cs = r[0:n_cols] + r[n_cols:2 * n_cols] + r[2 * n_cols:3 * n_cols] + cpre
        cpre = cpre + jnp.sum(lf, axis=1, keepdims=True)
        s_pages.append(s - cs * LOG2E)
    cpre_ref[...] = cpre
    fox_update(jnp.concatenate(s_pages, axis=1), [vf_buf[slot, j] for j in range(pp)])

    sd_pages = []
    for j in range(pp):
        dist = (past - ((c * pp + j) * page + lane)).astype(F32)
        sd_pages.append(scores(kd_buf[slot, j], qd_col) - slope_ref[...] * dist)
    alpha, p = softmax_step(jnp.concatenate(sd_pages, axis=1), md_ref, sd_ref)
    pb = p.astype(BF16)
    for hh in range(n_dh):
        acc = accd_ref[hh] * alpha
        for j in range(pp):
            vh = vd_buf[slot, j, pl.ds(hh, page, stride=n_dh), :].astype(BF16)
            acc = acc + jnp.dot(pb[:, j * LANES:(j + 1) * LANES], vh, preferred_element_type=F32)
        accd_ref[hh] = acc

    @pl.when(c == n_chunks - 1)
    def _():
        s_new = scores(pick(kfn_ref), qf_col)
        c_tot = cpre_ref[...] + pick(lfn_ref)
        s_new = jnp.where(first_lane, s_new - c_tot * LOG2E, NEG_INF)
        fox_update(s_new, [pick(vfn_ref)])
        sd_new = jnp.where(first_lane, scores(pick(kdn_ref), qd_col), NEG_INF)
        alpha, p = softmax_step(sd_new, md_ref, sd_ref)
        p_new = jnp.sum(p, axis=1, keepdims=True)
        for hh in range(n_dh):
            accd_ref[hh] = accd_ref[hh] * alpha + p_new * vdn_ref[hh:hh + 1, :]

        inv_f = 1.0 / sf_ref[...]
        for hh in range(n_cols):
            sl = slice(hh * HEAD_DIM, (hh + 1) * HEAD_DIM)
            col = jnp.sum(accf_ref[sl, :], axis=1, keepdims=True) * inv_f[hh:hh + 1, :]
            oft_ref[sl, :] = jnp.where(seq_lane, jnp.broadcast_to(col, (HEAD_DIM, LANES)), oft_ref[sl, :])
        inv_d = 1.0 / sd_ref[...]
        for hh in range(n_dh):
            a = accd_ref[hh] * inv_d
            o = a[2 * hh:2 * hh + 1, :] - lam_ref[0] * a[2 * hh + 1:2 * hh + 2, :]
            od_ref[hh:hh + 1, :] = _rms(o, gs_ref[...]) * out_scale


def _decode_attention(page_table, lam, caches, qft, qdt, new_cols, vdn, slope_col, tri, g_subln,
                      *, pages_per_step, out_scale):
    kf_c, vf_c, lf_c, kd_c, vd_c = caches
    kfn, vfn, lfn, kdn = new_cols
    n_seq, n_pages = page_table.shape
    rows, page = kf_c.shape[1], kf_c.shape[2]
    n_cols = lf_c.shape[1]
    n_dh = vd_c.shape[1] // page
    assert n_seq % LANES == 0 and page == LANES and n_pages % pages_per_step == 0
    pp = pages_per_step
    any_spec = pl.BlockSpec(memory_space=pl.ANY)
    col_spec = lambda r: pl.BlockSpec((r, LANES), lambda s, c, pt: (0, s // LANES))
    const2 = lambda r, c_: pl.BlockSpec((r, c_), lambda s, c, pt: (0, 0))
    grid_spec = pltpu.PrefetchScalarGridSpec(
        num_scalar_prefetch=1,
        grid=(n_seq, n_pages // pp),
        in_specs=[pl.BlockSpec(memory_space=pltpu.SMEM), any_spec, any_spec, any_spec, any_spec, any_spec,
                  col_spec(rows), col_spec(rows), col_spec(rows), col_spec(rows), col_spec(n_cols),
                  col_spec(rows), pl.BlockSpec((None, n_dh, LANES), lambda s, c, pt: (s, 0, 0)),
                  const2(n_cols, LANES), const2(page, page), const2(1, LANES)],
        out_specs=(col_spec(rows), pl.BlockSpec((None, n_dh, LANES), lambda s, c, pt: (s, 0, 0))),
        scratch_shapes=[pltpu.VMEM((2, pp, rows, page), F32), pltpu.VMEM((2, pp, rows, page), F32),
                        pltpu.VMEM((2, pp, n_cols, page), F32), pltpu.VMEM((2, pp, rows, page), F32),
                        pltpu.VMEM((2, pp, page * n_dh, LANES), F32), pltpu.SemaphoreType.DMA((2,)),
                        pltpu.VMEM((rows, LANES), F32), pltpu.VMEM((rows, LANES), F32),
                        pltpu.VMEM((n_cols, 1), F32), pltpu.VMEM((n_cols, 1), F32),
                        pltpu.VMEM((rows, LANES), F32), pltpu.VMEM((n_cols, 1), F32),
                        pltpu.VMEM((n_cols, 1), F32), pltpu.VMEM((n_cols, 1), F32),
                        pltpu.VMEM((n_dh, n_cols, LANES), F32)],
    )
    return pl.pallas_call(
        functools.partial(_decode_kernel, pp=pp, page=page, past=n_pages * page, out_scale=out_scale),
        grid_spec=grid_spec,
        out_shape=(jax.ShapeDtypeStruct((rows, n_seq), F32),
                   jax.ShapeDtypeStruct((n_seq, n_dh, LANES), F32)),
        compiler_params=_cparams(2),
        name="decode_attn",
    )(page_table.reshape(-1), lam, kf_c, vf_c, lf_c, kd_c, vd_c,
      qft, qdt, kfn, vfn, lfn, kdn, vdn, slope_col, tri, g_subln)


def _mix_kernel(o_ref, gate_ref, x_ref, wuf_ref, wud_ref, wo_ref, gn_ref, wr_ref, br_ref, base_ref,
                x1_ref, h2_ref, idx_ref, wgt_ref, rank_ref, cnt_ref, cnt_acc, *, n_experts):
    i = pl.program_id(0)
    tm = x_ref.shape[0]
    half = o_ref.shape[1] // 2
    d = x_ref.shape[1]
    up_f = jnp.dot(o_ref[:, :half], wuf_ref[...], preferred_element_type=F32)
    up_d = jnp.dot(o_ref[:, half:], wud_ref[...], preferred_element_type=F32)
    merged = gate_ref[:, :d].astype(F32) * up_f + gate_ref[:, d:].astype(F32) * up_d
    x1 = x_ref[...] + jnp.dot(merged.astype(BF16), wo_ref[...], preferred_element_type=F32)
    x1_ref[...] = x1
    h2 = _rms(x1, gn_ref[...])
    h2_ref[...] = h2

    logits = jnp.dot(h2, wr_ref[...], precision=HI, preferred_element_type=F32) + br_ref[...]
    lane = lax.broadcasted_iota(I32, logits.shape, 1).astype(F32)
    out_lane = lax.broadcasted_iota(I32, (tm, TOP_K), 1)
    remaining = logits
    picks, vals = [], []
    for _ in range(TOP_K):
        mx = jnp.max(remaining, axis=1, keepdims=True)
        sel = jnp.min(jnp.where(remaining == mx, lane, float(n_experts)), axis=1, keepdims=True)
        hit = lane == sel
        picks.append(hit)
        vals.append(mx)
        remaining = jnp.where(hit, NEG_INF, remaining)
    exps = [jnp.exp(v - vals[0]) for v in vals]
    denom = exps[0] + exps[1] + exps[2] + exps[3]

    @pl.when(i == 0)
    def _():
        cnt_acc[...] = base_ref[...]

    chosen = jnp.zeros(logits.shape, F32)
    for hit in picks:
        chosen = chosen + jnp.where(hit, 1.0, 0.0)
    row = lax.broadcasted_iota(I32, (tm, tm), 0)
    col = lax.broadcasted_iota(I32, (tm, tm), 1)
    strict = jnp.where(col < row, 1.0, 0.0).astype(BF16)
    before = jnp.dot(strict, chosen.astype(BF16), preferred_element_type=F32) + cnt_acc[...]
    cnt_acc[...] = cnt_acc[...] + jnp.sum(chosen, axis=0, keepdims=True)
    cnt_ref[...] = cnt_acc[...]

    idx_out = jnp.zeros((tm, TOP_K), I32)
    wgt_out = jnp.zeros((tm, TOP_K), F32)
    rank_out = jnp.zeros((tm, TOP_K), I32)
    for kk in range(TOP_K):
        sel = jnp.sum(jnp.where(picks[kk], lane, 0.0), axis=1, keepdims=True).astype(I32)
        rk = jnp.sum(jnp.where(picks[kk], before, 0.0), axis=1, keepdims=True).astype(I32)
        idx_out = jnp.where(out_lane == kk, sel, idx_out)
        wgt_out = jnp.where(out_lane == kk, exps[kk] / denom, wgt_out)
        rank_out = jnp.where(out_lane == kk, rk, rank_out)
    idx_ref[...] = idx_out
    wgt_ref[...] = wgt_out
    rank_ref[...] = rank_out


def _mix(o, gates, x, wuf, wud, wo, g_norm, w_router, b_router, base, *, tm):
    m, d = x.shape
    n_exp = w_router.shape[1]
    row_spec = lambda n: pl.BlockSpec((tm, n), lambda i: (i, 0))
    return pl.pallas_call(
        functools.partial(_mix_kernel, n_experts=n_exp),
        grid=(m // tm,),
        in_specs=[row_spec(o.shape[1]), row_spec(gates.shape[1]), row_spec(d),
                  _const_spec(wuf.shape), _const_spec(wud.shape), _const_spec(wo.shape),
                  _const_spec((1, d)), _const_spec(w_router.shape), _const_spec((1, n_exp)),
                  _const_spec((1, n_exp))],
        out_specs=(row_spec(d), row_spec(d), row_spec(TOP_K), row_spec(TOP_K), row_spec(TOP_K),
                   pl.BlockSpec((1, n_exp), lambda i: (0, 0))),
        out_shape=(jax.ShapeDtypeStruct((m, d), F32), jax.ShapeDtypeStruct((m, d), F32),
                   jax.ShapeDtypeStruct((m, TOP_K), I32), jax.ShapeDtypeStruct((m, TOP_K), F32),
                   jax.ShapeDtypeStruct((m, TOP_K), I32), jax.ShapeDtypeStruct((1, n_exp), F32)),
        scratch_shapes=[pltpu.VMEM((1, n_exp), F32)],
        compiler_params=_cparams(1),
        name="mix_router",
    )(o, gates, x, wuf, wud, wo, g_norm, w_router, b_router, base)


def _row_copy(src, src_row, dst, dst_row, sem):
    return pltpu.make_async_copy(src.at[pl.ds(src_row, 1), :], dst.at[pl.ds(dst_row, 1), :], sem)


def _dispatch_kernel(dest_ref, h_ref, rows_in_ref, rows_ref, sem, *, tc):
    del rows_in_ref

    def issue(t, carry):
        for kk in range(TOP_K):
            _row_copy(h_ref, t, rows_ref, dest_ref[t * TOP_K + kk], sem).start()
        return carry

    lax.fori_loop(0, tc, issue, 0, unroll=8)
    for kk in range(TOP_K):
        pltpu.make_async_copy(h_ref, rows_ref.at[pl.ds(0, tc), :], sem).wait()


def _dispatch(dest_flat, h, rows, *, tc):
    m, d = h.shape
    return pl.pallas_call(
        functools.partial(_dispatch_kernel, tc=tc),
        grid=(m // tc,),
        in_specs=[pl.BlockSpec((tc * TOP_K,), lambda i: (i,), memory_space=pltpu.SMEM),
                  pl.BlockSpec((tc, d), lambda i: (i, 0)), pl.BlockSpec(memory_space=pl.ANY)],
        out_specs=pl.BlockSpec(memory_space=pl.ANY),
        out_shape=jax.ShapeDtypeStruct(rows.shape, rows.dtype),
        scratch_shapes=[pltpu.SemaphoreType.DMA(())],
        input_output_aliases={2: 0},
        compiler_params=_cparams(1, has_side_effects=True),
        name="dispatch",
    )(dest_flat, h, rows)


def _combine_kernel(dest_ref, y_ref, w_ref, x1_ref, g_ref, o_ref, buf, sem, *, tc, final_norm):
    def issue(t, carry):
        for kk in range(TOP_K):
            _row_copy(y_ref, dest_ref[t * TOP_K + kk], buf.at[kk], t, sem).start()
        return carry

    lax.fori_loop(0, tc, issue, 0, unroll=8)
    for kk in range(TOP_K):
        pltpu.make_async_copy(y_ref.at[pl.ds(0, tc), :], buf.at[kk], sem).wait()
    out = x1_ref[...]
    for kk in range(TOP_K):
        out = out + buf[kk] * w_ref[:, kk:kk + 1]
    if final_norm:
        out = _rms(out, g_ref[...])
    o_ref[...] = out


def _combine(dest_flat, y_rows, wgt, x1, g, *, tc, final_norm):
    m, d = x1.shape
    return pl.pallas_call(
        functools.partial(_combine_kernel, tc=tc, final_norm=final_norm),
        grid=(m // tc,),
        in_specs=[pl.BlockSpec((tc * TOP_K,), lambda i: (i,), memory_space=pltpu.SMEM),
                  pl.BlockSpec(memory_space=pl.ANY),
                  pl.BlockSpec((tc, TOP_K), lambda i: (i, 0)),
                  pl.BlockSpec((tc, d), lambda i: (i, 0)),
                  pl.BlockSpec((1, d), lambda i: (0, 0))],
        out_specs=pl.BlockSpec((tc, d), lambda i: (i, 0)),
        out_shape=jax.ShapeDtypeStruct((m, d), F32),
        scratch_shapes=[pltpu.VMEM((TOP_K, tc, d), F32), pltpu.SemaphoreType.DMA(())],
        compiler_params=_cparams(1),
        name="combine",
    )(dest_flat, y_rows, wgt, x1, g)


def _expert_kernel(be_ref, na_ref, x_ref, wgu_ref, bgu_ref, wd_ref, bd_ref, y_ref, wgu_bf, wd_bf):
    i = pl.program_id(0)
    d_ff = wd_ref.shape[0]
    active = i < na_ref[0]
    new_expert = jnp.logical_or(i == 0, be_ref[i] != be_ref[jnp.maximum(i - 1, 0)])

    @pl.when(jnp.logical_and(active, new_expert))
    def _():
        wgu_bf[...] = wgu_ref[...].astype(BF16)
        wd_bf[...] = wd_ref[...].astype(BF16)

    @pl.when(active)
    def _():
        gu = jnp.dot(x_ref[...].astype(BF16), wgu_bf[...], preferred_element_type=F32) + bgu_ref[...]
        glu = jnp.minimum(gu[:, :d_ff], SWIGLU_LIMIT)
        lin = jnp.clip(gu[:, d_ff:], -SWIGLU_LIMIT, SWIGLU_LIMIT)
        act = glu * (1.0 / (1.0 + jnp.exp(-SWIGLU_ALPHA * glu))) * (lin + 1.0)
        y_ref[...] = jnp.dot(act.astype(BF16), wd_bf[...], preferred_element_type=F32) + bd_ref[...]

    @pl.when(i >= na_ref[0])
    def _():
        y_ref[...] = jnp.zeros_like(y_ref)


def _experts(block_expert, n_active, x_rows, wgu, bgu, wd, bd):
    n_rows, d = x_rows.shape
    n_blocks = n_rows // ROW_BLOCK
    grid_spec = pltpu.PrefetchScalarGridSpec(
        num_scalar_prefetch=2,
        grid=(n_blocks,),
        in_specs=[pl.BlockSpec((ROW_BLOCK, d), lambda i, be, na: (i, 0)),
                  pl.BlockSpec((None,) + wgu.shape[1:], lambda i, be, na: (be[i], 0, 0)),
                  pl.BlockSpec((None,) + bgu.shape[1:], lambda i, be, na: (be[i], 0, 0)),
                  pl.BlockSpec((None,) + wd.shape[1:], lambda i, be, na: (be[i], 0, 0)),
                  pl.BlockSpec((None,) + bd.shape[1:], lambda i, be, na: (be[i], 0, 0))],
        out_specs=pl.BlockSpec((ROW_BLOCK, d), lambda i, be, na: (i, 0)),
        scratch_shapes=[pltpu.VMEM(wgu.shape[1:], BF16), pltpu.VMEM(wd.shape[1:], BF16)],
    )
    return pl.pallas_call(
        _expert_kernel,
        grid_spec=grid_spec,
        out_shape=jax.ShapeDtypeStruct((n_rows, d), F32),
        compiler_params=_cparams(1),
        name="experts",
    )(block_expert, n_active, x_rows, wgu, bgu, wd, bd)


def kernel(x_prompt, x_sample, cache_fox_k, cache_fox_v, cache_fox_logf, cache_diff_k, cache_diff_v,
           page_table, g_attn_norm, w_in, b_forget, b_gate, lambda_q1, lambda_k1, lambda_q2, lambda_k2,
           g_subln, w_up_fox, w_up_diff, w_out, g_ffn_norm, w_router, b_router, w_gate_up, b_gate_up,
           w_down, b_down, g_final):
    depth = w_in.shape[0]
    b, t, d = x_prompt.shape
    n_seq = x_sample.shape[0]
    assert x_sample.shape[1] == 1
    n_phys, page, n_fox = cache_fox_k.shape[1:4]
    n_diff = cache_diff_k.shape[3]
    width = n_fox * HEAD_DIM
    assert 2 * n_diff == n_fox and cache_diff_v.shape[4] == 2 * HEAD_DIM
    n_exp = w_router.shape[2]
    m_p = b * t
    slopes = 2.0 ** (-8.0 * jnp.arange(1, n_diff + 1, dtype=F32) / n_diff)
    slope_l2 = (jnp.repeat(slopes, 2) * LOG2E)
    slope_col = jnp.broadcast_to(slope_l2[:, None], (n_fox, LANES))
    pq, pk = _bias_placement(2 * n_fox)
    tri_u = (jnp.arange(page)[:, None] <= jnp.arange(page)[None, :]).astype(BF16)
    tm_p = min(TOKEN_BLOCK, t)

    xp = x_prompt.reshape(m_p, d)
    xs = x_sample.reshape(n_seq, d)
    outs_p, outs_s = [], []
    for l in range(depth):
        lam_init = 0.8 - 0.6 * math.exp(-0.3 * l)
        lam = (jnp.exp(jnp.sum(lambda_q1[l].astype(F32) * lambda_k1[l].astype(F32)))
               - jnp.exp(jnp.sum(lambda_q2[l].astype(F32) * lambda_k2[l].astype(F32)))
               + lam_init).reshape(1)
        out_scale = 1.0 - lam_init
        w = w_in[l]
        o_fq, o_fk, o_fv, o_zf = 0, width, 2 * width, 3 * width
        o_dq = o_zf + n_fox
        o_dk, o_dv, o_zg = o_dq + width, o_dq + 2 * width, o_dq + 3 * width
        sl = lambda o, n=width: w[:, o:o + n]
        w_nn = jnp.concatenate([sl(o_fq), sl(o_dq), sl(o_fk), sl(o_dk), sl(o_dv), w[:, o_zg:]],
                               axis=1).astype(BF16)
        w_nt = jnp.concatenate([sl(o_fk), sl(o_dk), sl(o_fv), sl(o_dv)], axis=1).T.astype(BF16)
        w_f = sl(o_zf, n_fox)
        g_attn = g_attn_norm[l][None]
        b_f = b_forget[l][None]
        b_g = b_gate[l][None]
        gs = g_subln[l][None]
        wuf, wud, wo = (w_up_fox[l].astype(BF16), w_up_diff[l].astype(BF16), w_out[l].astype(BF16))
        g_ffn = g_ffn_norm[l][None]
        last = l == depth - 1
        g_out = g_final[None] if last else jnp.ones((1, d), F32)
        proj = functools.partial(_proj, g=g_attn, w_nn=w_nn, w_nt=w_nt, w_f=w_f, b_f=b_f, b_g=b_g,
                                 slope_l2=slope_l2[None], pq=pq, pk=pk)

        (q_p, qb_p, k_p, kb_p, vt_p, fkt_p, fvt_p, dkt_p, dv_p, lft_p, gate_p) = proj(
            xp, tm=tm_p, n_batch=b)
        r3 = lambda a: a.reshape(b, t, a.shape[-1])
        o_p = _prompt_attention(lam, r3(q_p), r3(qb_p), r3(k_p), r3(kb_p), vt_p, gs,
                                blk=tm_p, n_fox_units=n_fox // 2, out_scale=out_scale)
        x1_p, h2_p, idx_p, wgt_p, rank_p, cnt_p = _mix(
            o_p.reshape(m_p, 2 * width), gate_p, xp, wuf, wud, wo, g_ffn, w_router[l], b_router[l][None],
            jnp.zeros((1, n_exp), F32), tm=tm_p)

        (q_s, _, _, _, _, fkt_s, fvt_s, dkt_s, dv_s, lft_s, gate_s) = proj(xs, tm=n_seq, n_batch=1)
        caches = (jnp.transpose(cache_fox_k[l], (0, 2, 3, 1)).reshape(n_phys, width, page),
                  jnp.transpose(cache_fox_v[l], (0, 2, 3, 1)).reshape(n_phys, width, page),
                  jnp.transpose(cache_fox_logf[l], (0, 2, 1)),
                  jnp.transpose(cache_diff_k[l], (0, 2, 3, 4, 1)).reshape(n_phys, width, page),
                  cache_diff_v[l].reshape(n_phys, page * n_diff, 2 * HEAD_DIM))
        q_t = q_s.astype(F32).T
        oft, od = _decode_attention(
            page_table, lam, caches, q_t[:width], q_t[width:],
            (fkt_s[0], fvt_s[0], lft_s[0], dkt_s[0]), dv_s.reshape(n_seq, n_diff, 2 * HEAD_DIM),
            slope_col, tri_u, gs, pages_per_step=4, out_scale=out_scale)
        o_s = jnp.concatenate([oft.T, od.reshape(n_seq, width)], axis=1).astype(BF16)
        x1_s, h2_s, idx_s, wgt_s, rank_s, cnt_all = _mix(
            o_s, gate_s, xs, wuf, wud, wo, g_ffn, w_router[l], b_router[l][None], cnt_p, tm=n_seq)

        counts = cnt_all[0].astype(I32)
        padded = (counts + ROW_BLOCK - 1) // ROW_BLOCK * ROW_BLOCK
        pad_end = jnp.cumsum(padded)
        pad_start = pad_end - padded
        n_assign = (m_p + n_seq) * TOP_K
        n_blocks = -(-n_assign // ROW_BLOCK) + n_exp
        block_start = jnp.arange(n_blocks, dtype=I32) * ROW_BLOCK
        block_expert = jnp.minimum(jnp.sum((pad_end[None, :] <= block_start[:, None]).astype(I32), axis=1),
                                   n_exp - 1).astype(I32)
        n_active = (pad_end[-1:] // ROW_BLOCK).astype(I32)
        dest_p = (pad_start[idx_p] + rank_p).reshape(-1)
        dest_s = (pad_start[idx_s] + rank_s).reshape(-1)

        x_rows = jnp.zeros((n_blocks * ROW_BLOCK, d), F32)
        x_rows = _dispatch(dest_p, h2_p, x_rows, tc=2 * ROUTE_BLOCK)
        x_rows = _dispatch(dest_s, h2_s, x_rows, tc=ROUTE_BLOCK)
        y_rows = _experts(block_expert, n_active, x_rows, w_gate_up[l], b_gate_up[l][:, None],
                          w_down[l], b_down[l][:, None])
        xp = _combine(dest_p, y_rows, wgt_p, x1_p, g_out, tc=ROUTE_BLOCK, final_norm=last)
        xs = _combine(dest_s, y_rows, wgt_s, x1_s, g_out, tc=ROUTE_BLOCK, final_norm=last)

        def kv_rows(kt_, n_b, n_t):
            return jnp.transpose(kt_.reshape(n_b, -1, HEAD_DIM, n_t), (0, 3, 1, 2))

        def cache_rows(fkt_, fvt_, lft_, dkt_, dv_, n_b, n_t):
            return (kv_rows(fkt_, n_b, n_t), kv_rows(fvt_, n_b, n_t), jnp.transpose(lft_, (0, 2, 1)),
                    kv_rows(dkt_, n_b, n_t).reshape(n_b, n_t, n_diff, 2, HEAD_DIM),
                    dv_.reshape(n_b, n_t, n_diff, 2 * HEAD_DIM))

        outs_p.append(cache_rows(fkt_p, fvt_p, lft_p, dkt_p, dv_p, b, t))
        rows_s = cache_rows(fkt_s, fvt_s, lft_s, dkt_s, dv_s, 1, n_seq)
        outs_s.append(tuple(a.reshape((n_seq, 1) + a.shape[2:]) for a in rows_s))

    stack = lambda rows: [jnp.stack(r) for r in zip(*rows)]
    return (xp.reshape(b, t, d), xs.reshape(n_seq, 1, d), *stack(outs_p), *stack(outs_s))
```

```python
import functools
import math

import numpy as np
import jax
import jax.numpy as jnp
from jax import lax
from jax.experimental import pallas as pl
from jax.experimental.pallas import tpu as pltpu

F32 = jnp.float32
BF16 = jnp.bfloat16
I32 = jnp.int32

HEAD_DIM = 64
LANES = 128
EPS = 1e-5
SWIGLU_LIMIT = 7.0
SWIGLU_ALPHA = 1.702
TOP_K = 4
ROW_BLOCK = 256
ROUTE_BLOCK = 128
TOKEN_BLOCK = 512
DECODE_PAGES = 16
VMEM_LIMIT = 56 * 1024 * 1024
NEG_INF = float("-inf")
LOG2E = math.log2(math.e)
HI = lax.Precision.HIGHEST
NT_DIMS = (((1,), (1,)), ((), ()))


def _cparams(n_axes, **kw):
    return pltpu.CompilerParams(dimension_semantics=("arbitrary",) * n_axes,
                                vmem_limit_bytes=VMEM_LIMIT, **kw)


def _const_spec(shape):
    nd = len(shape)
    return pl.BlockSpec(shape, lambda *_: (0,) * nd, pipeline_mode=pl.Buffered(1))


def _trunc_bf16(x):
    bits = pltpu.bitcast(x, jnp.uint32) & jnp.uint32(0xFFFF0000)
    return pltpu.bitcast(bits, F32)


def _split3(x):
    hi = _trunc_bf16(x)
    r = x - hi
    mid = _trunc_bf16(r)
    return hi, mid, r - mid


def _rms(x, g):
    ms = jnp.mean(x * x, axis=-1, keepdims=True)
    return x * lax.rsqrt(ms + EPS) * g


def _proj_kernel(x_ref, g_ref, w_ref, wt_ref, wf_ref, wft_ref, bf_ref, bft_ref, bg_ref, slope_ref,
                 pq_ref, pk_ref,
                 q_ref, qb_ref, k_ref, kb_ref, vt_ref, fkt_ref, fvt_ref, dkt_ref, dv_ref, lft_ref, gate_ref,
                 carry_ref, *, blocks_per_seq, width):
    i = pl.program_id(0)
    tm = x_ref.shape[0]
    h = _rms(x_ref[...], g_ref[...])
    hb = h.astype(BF16)
    scale = HEAD_DIM ** -0.5 * LOG2E

    def proj(col):
        return jnp.dot(hb, w_ref[:, col * width:(col + 1) * width], preferred_element_type=F32)

    def proj_t(grp):
        return lax.dot_general(wt_ref[grp * width:(grp + 1) * width, :], hb, NT_DIMS,
                               preferred_element_type=F32)

    q_ref[:, :width] = (proj(0) * scale).astype(BF16)
    q_ref[:, width:] = (proj(1) * scale).astype(BF16)
    k_ref[:, :width] = proj(2).astype(BF16)
    k_ref[:, width:] = proj(3).astype(BF16)
    dv_ref[...] = proj(4)
    fkt_ref[...] = proj_t(0)
    dkt_ref[...] = proj_t(1)
    fvt = proj_t(2)
    fvt_ref[...] = fvt
    vt_ref[:width, :] = fvt.astype(BF16)
    vt_ref[width:, :] = proj_t(3).astype(BF16)
    n_gate = gate_ref.shape[1] // width
    for j in range(n_gate):
        z = proj(5 + j) + bg_ref[:, j * width:(j + 1) * width]
        gate_ref[:, j * width:(j + 1) * width] = (1.0 / (1.0 + jnp.exp(-z))).astype(BF16)

    def log_sigmoid(z):
        return jnp.minimum(z, 0.0) - jnp.log1p(jnp.exp(-jnp.abs(z)))

    zft = lax.dot_general(wft_ref[...], h, NT_DIMS, precision=HI, preferred_element_type=F32)
    lft_ref[...] = log_sigmoid(zft + bft_ref[...])
    zf = jnp.dot(h, wf_ref[...], precision=HI, preferred_element_type=F32)
    lf = log_sigmoid(zf + bf_ref[...])

    @pl.when(i % blocks_per_seq == 0)
    def _():
        carry_ref[...] = jnp.zeros_like(carry_ref)

    row = lax.broadcasted_iota(I32, (tm, tm), 0)
    col = lax.broadcasted_iota(I32, (tm, tm), 1)
    tri = jnp.where(col <= row, 1.0, 0.0).astype(BF16)
    c = carry_ref[...]
    for part in _split3(lf):
        c = c + jnp.dot(tri, part.astype(BF16), preferred_element_type=F32)
    carry_ref[...] = c[tm - 1:tm, :]

    pos = ((i % blocks_per_seq) * tm + lax.broadcasted_iota(I32, (tm, 1), 0)).astype(F32)
    a = jnp.concatenate([c * LOG2E, -(slope_ref[...] * pos)], axis=1)
    hi, mid, lo = _split3(a)
    pieces = jnp.concatenate([hi, mid, lo, jnp.ones_like(a)], axis=1).astype(BF16)
    qb_ref[...] = jnp.dot(pieces, pq_ref[...], preferred_element_type=F32).astype(BF16)
    kb_ref[...] = jnp.dot(pieces, pk_ref[...], preferred_element_type=F32).astype(BF16)


def _proj(x, g, w_nn, w_nt, w_f, b_f, b_g, slope_l2, pq, pk, *, tm, n_batch):
    m, d = x.shape
    width = w_nt.shape[0] // 4
    n_f = w_f.shape[1]
    gate_w = b_g.shape[1]
    t = m // n_batch
    bps = t // tm
    row_spec = lambda n: pl.BlockSpec((tm, n), lambda i: (i, 0))
    tr_spec = lambda r: pl.BlockSpec((None, r, tm), lambda i: (i // bps, 0, i % bps))
    out_shape = (
        jax.ShapeDtypeStruct((m, 2 * width), BF16),
        jax.ShapeDtypeStruct((m, 2 * width), BF16),
        jax.ShapeDtypeStruct((m, 2 * width), BF16),
        jax.ShapeDtypeStruct((m, 2 * width), BF16),
        jax.ShapeDtypeStruct((m // tm, 2 * width, tm), BF16),
        jax.ShapeDtypeStruct((n_batch, width, t), F32),
        jax.ShapeDtypeStruct((n_batch, width, t), F32),
        jax.ShapeDtypeStruct((n_batch, width, t), F32),
        jax.ShapeDtypeStruct((m, width), F32),
        jax.ShapeDtypeStruct((n_batch, n_f, t), F32),
        jax.ShapeDtypeStruct((m, gate_w), BF16),
    )
    out_specs = (row_spec(2 * width), row_spec(2 * width), row_spec(2 * width), row_spec(2 * width),
                 pl.BlockSpec((None, 2 * width, tm), lambda i: (i, 0, 0)),
                 tr_spec(width), tr_spec(width), tr_spec(width), row_spec(width), tr_spec(n_f),
                 row_spec(gate_w))
    consts = (g, w_nn, w_nt, w_f, w_f.T, b_f, b_f.T, b_g, slope_l2, pq, pk)
    return pl.pallas_call(
        functools.partial(_proj_kernel, blocks_per_seq=bps, width=width),
        grid=(m // tm,),
        in_specs=[row_spec(d)] + [_const_spec(c.shape) for c in consts],
        out_specs=out_specs,
        out_shape=out_shape,
        scratch_shapes=[pltpu.VMEM((1, n_f), F32)],
        compiler_params=_cparams(1),
        name="proj",
    )(x, *consts)


def _bias_placement(n_cols):
    pq = np.zeros((4 * n_cols, n_cols * HEAD_DIM), np.float32)
    pk = np.zeros_like(pq)
    for j in range(n_cols):
        base = j * HEAD_DIM
        for part in range(3):
            pq[part * n_cols + j, base + part] = 1.0
            pq[3 * n_cols + j, base + 3 + part] = 1.0
            pk[3 * n_cols + j, base + part] = 1.0
            pk[part * n_cols + j, base + 3 + part] = -1.0
    return jnp.asarray(pq, BF16), jnp.asarray(pk, BF16)


def _attn_kernel(lam_ref, q_ref, qb_ref, k_ref, kb_ref, vt_ref, gs_ref, o_ref,
                 acc_ref, m_ref, s0_ref, s1_ref, *, blk, n_fox_units, out_scale):
    u = pl.program_id(1)
    qi = pl.program_id(2)
    q2 = jnp.concatenate([q_ref[...], qb_ref[...]], axis=1)
    lane2 = lax.broadcasted_iota(I32, q2.shape, 1)
    first_half = (lane2 & (LANES - 1)) < HEAD_DIM
    zero = jnp.zeros_like(q2)
    q_both = jnp.concatenate([jnp.where(first_half, q2, zero), jnp.where(first_half, zero, q2)], axis=0)
    ones_rows = jnp.ones((16, blk), BF16)

    m_ref[...] = jnp.full_like(m_ref, NEG_INF)
    acc_ref[...] = jnp.zeros_like(acc_ref)

    def scores(kj, st_ref):
        off = pl.multiple_of(kj * blk, blk)
        k2 = jnp.concatenate([k_ref[pl.ds(off, blk), :], kb_ref[pl.ds(off, blk), :]], axis=1)
        st_ref[...] = lax.dot_general(k2, q_both, NT_DIMS, preferred_element_type=F32)

    def accumulate(kj, st_ref, masked):
        vt = jnp.concatenate([vt_ref[kj], ones_rows], axis=0)
        st = st_ref[...]
        if masked:
            krow = lax.broadcasted_iota(I32, st.shape, 0)
            qcol = lax.broadcasted_iota(I32, st.shape, 1) & (blk - 1)
            st = jnp.where(krow <= qcol, st, NEG_INF)
        m_old = m_ref[...]
        m_new = jnp.maximum(m_old, jnp.max(st, axis=0, keepdims=True))
        alpha = jnp.exp2(m_old - m_new)
        p = jnp.exp2((st - m_new).astype(BF16))
        acc_ref[...] = acc_ref[...] * alpha + jnp.dot(vt, p, preferred_element_type=F32)
        m_ref[...] = m_new

    scores(0, s0_ref)

    def pair(jj, carry):
        scores(2 * jj + 1, s1_ref)
        accumulate(2 * jj, s0_ref, False)
        scores(2 * jj + 2, s0_ref)
        accumulate(2 * jj + 1, s1_ref, False)
        return carry

    lax.fori_loop(0, qi // 2, pair, 0)

    @pl.when(qi % 2 == 0)
    def _():
        accumulate(qi, s0_ref, True)

    @pl.when(qi % 2 == 1)
    def _():
        scores(qi, s1_ref)
        accumulate(qi - 1, s0_ref, False)
        accumulate(qi, s1_ref, True)

    acc = acc_ref[...]
    o_a = acc[:LANES, :blk] / acc[LANES:LANES + 1, :blk]
    o_b = acc[:LANES, blk:] / acc[LANES:LANES + 1, blk:]

    @pl.when(u < n_fox_units)
    def _():
        r = lax.broadcasted_iota(I32, o_a.shape, 0)
        o_ref[...] = jnp.where(r < HEAD_DIM, o_a, o_b).T.astype(BF16)

    @pl.when(u >= n_fox_units)
    def _():
        o = (o_a - lam_ref[0] * o_b).T
        o_ref[...] = (_rms(o, gs_ref[...]) * out_scale).astype(BF16)


def _prompt_attention(lam, q, qb, k, kb, vt, g_subln, *, blk, n_fox_units, out_scale):
    b, t, w = q.shape
    n_units = w // LANES
    nkb = t // blk
    q_spec = pl.BlockSpec((None, blk, LANES), lambda bi, u, qi: (bi, qi, u))
    k_spec = pl.BlockSpec((None, t, LANES), lambda bi, u, qi: (bi, 0, u))
    vt_spec = pl.BlockSpec((nkb, LANES, blk), lambda bi, u, qi: (bi, u, 0))
    return pl.pallas_call(
        functools.partial(_attn_kernel, blk=blk, n_fox_units=n_fox_units, out_scale=out_scale),
        grid=(b, n_units, nkb),
        in_specs=[pl.BlockSpec(memory_space=pltpu.SMEM), q_spec, q_spec, k_spec, k_spec, vt_spec,
                  pl.BlockSpec((1, LANES), lambda bi, u, qi: (0, 0))],
        out_specs=q_spec,
        out_shape=jax.ShapeDtypeStruct((b, t, w), BF16),
        scratch_shapes=[pltpu.VMEM((LANES + 16, 2 * blk), F32), pltpu.VMEM((1, 2 * blk), F32),
                        pltpu.VMEM((blk, 2 * blk), F32), pltpu.VMEM((blk, 2 * blk), F32)],
        compiler_params=_cparams(3),
        name="prompt_attn",
    )(lam, q, qb, k, kb, vt, g_subln)


def _decode_kernel(pt_ref, lam_ref, kf_hbm, vf_hbm, lf_hbm, kd_hbm, vd_hbm,
                   qft_ref, qdt_ref, kfn_ref, vfn_ref, lfn_ref, kdn_ref, vdn_ref, slope_ref, tri_ref, gs_ref,
                   oft_ref, od_ref,
                   kf_buf, vf_buf, lf_buf, kd_buf, vd_buf, sem,
                   qf_col, qd_col, mf_ref, sf_ref, accf_ref, cpre_ref, md_ref, sd_ref, accd_ref,
                   *, pp, page, past, out_scale):
    s_idx = pl.program_id(0)
    c = pl.program_id(1)
    n_chunks = pl.num_programs(1)
    n = s_idx * n_chunks + c
    total = pl.num_programs(0) * n_chunks
    slot = n % 2
    n_cols = mf_ref.shape[0]
    rows = kf_buf.shape[2]
    n_dh = accd_ref.shape[0]
    lane = lax.broadcasted_iota(I32, (1, LANES), 1)
    seq_lane = lane == s_idx % LANES

    def chunk_copies(chunk, slot_):
        cps = []
        for j in range(pp):
            pg = pt_ref[chunk * pp + j]
            for hbm, buf in ((kf_hbm, kf_buf), (vf_hbm, vf_buf), (lf_hbm, lf_buf), (kd_hbm, kd_buf),
                             (vd_hbm, vd_buf)):
                cps.append(pltpu.make_async_copy(hbm.at[pg], buf.at[slot_, j], sem.at[slot_]))
        return cps

    @pl.when(n == 0)
    def _():
        for cp in chunk_copies(0, 0):
            cp.start()

    @pl.when(n + 1 < total)
    def _():
        for cp in chunk_copies(n + 1, 1 - slot):
            cp.start()

    def pick(ref):
        col = jnp.sum(jnp.where(seq_lane, ref[...], 0.0), axis=1, keepdims=True)
        return jnp.broadcast_to(col, (ref.shape[0], LANES))

    @pl.when(c == 0)
    def _():
        qf_col[...] = pick(qft_ref)
        qd_col[...] = pick(qdt_ref)
        mf_ref[...] = jnp.full_like(mf_ref, NEG_INF)
        md_ref[...] = jnp.full_like(md_ref, NEG_INF)
        sf_ref[...] = jnp.zeros_like(sf_ref)
        sd_ref[...] = jnp.zeros_like(sd_ref)
        accf_ref[...] = jnp.zeros_like(accf_ref)
        accd_ref[...] = jnp.zeros_like(accd_ref)
        cpre_ref[...] = jnp.zeros_like(cpre_ref)

    @pl.when(jnp.logical_and(c == 0, s_idx % LANES == 0))
    def _():
        oft_ref[...] = jnp.zeros_like(oft_ref)

    def scores(kt, qcol_ref):
        out = []
        for hh in range(n_cols):
            sl = slice(hh * HEAD_DIM, (hh + 1) * HEAD_DIM)
            out.append(jnp.sum(kt[sl, :] * qcol_ref[sl, :], axis=0, keepdims=True))
        return jnp.concatenate(out, axis=0)

    def softmax_step(s, m_ref, s_ref):
        m_old = m_ref[...]
        m_new = jnp.maximum(m_old, jnp.max(s, axis=1, keepdims=True))
        alpha = jnp.exp2(m_old - m_new)
        p = jnp.exp2(s - m_new)
        s_ref[...] = alpha * s_ref[...] + jnp.sum(p, axis=1, keepdims=True)
        m_ref[...] = m_new
        return alpha, p

    def fox_update(s, vts):
        alpha, p = softmax_step(s, mf_ref, sf_ref)
        for hh in range(n_cols):
            sl = slice(hh * HEAD_DIM, (hh + 1) * HEAD_DIM)
            acc = accf_ref[sl, :] * alpha[hh:hh + 1, :]
            for j, vt in enumerate(vts):
                acc = acc + vt[sl, :] * p[hh:hh + 1, j * LANES:(j + 1) * LANES]
            accf_ref[sl, :] = acc

    first_lane = lane == 0

    for cp in chunk_copies(n, slot):
        cp.wait()

    s_pages = []
    cpre = cpre_ref[...]
    for j in range(pp):
        s = scores(kf_buf[slot, j], qf_col)
        lf = lf_buf[slot, j]
        hi, mid, lo = _split3(lf)
        parts = jnp.concatenate([hi, mid, lo, jnp.zeros_like(lf)], axis=0).astype(BF16)
        r = jnp.dot(parts, tri_ref[...], preferred_element_type=F32)
        cs = r[0:n_cols] + r[n_cols:2 * n_cols] + r[2 * n_cols:3 * n_cols] + cpre
        cpre = cpre + jnp.sum(lf, axis=1, keepdims=True)
        s_pages.append(s - cs * LOG2E)
    cpre_ref[...] = cpre
    fox_update(jnp.concatenate(s_pages, axis=1), [vf_buf[slot, j] for j in range(pp)])

    sd_pages = []
    for j in range(pp):
        dist = (past - ((c * pp + j) * page + lane)).astype(F32)
        sd_pages.append(scores(kd_buf[slot, j], qd_col) - slope_ref[...] * dist)
    alpha, p = softmax_step(jnp.concatenate(sd_pages, axis=1), md_ref, sd_ref)
    pb = p.astype(BF16)
    for hh in range(n_dh):
        acc = accd_ref[hh] * alpha
        for j in range(pp):
            vh = vd_buf[slot, j, pl.ds(hh, page, stride=n_dh), :].astype(BF16)
            acc = acc + jnp.dot(pb[:, j * LANES:(j + 1) * LANES], vh, preferred_element_type=F32)
        accd_ref[hh] = acc

    @pl.when(c == n_chunks - 1)
    def _():
        s_new = scores(pick(kfn_ref), qf_col)
        c_tot = cpre_ref[...] + pick(lfn_ref)
        s_new = jnp.where(first_lane, s_new - c_tot * LOG2E, NEG_INF)
        fox_update(s_new, [pick(vfn_ref)])
        sd_new = jnp.where(first_lane, scores(pick(kdn_ref), qd_col), NEG_INF)
        alpha, p = softmax_step(sd_new, md_ref, sd_ref)
        p_new = jnp.sum(p, axis=1, keepdims=True)
        for hh in range(n_dh):
            accd_ref[hh] = accd_ref[hh] * alpha + p_new * vdn_ref[hh:hh + 1, :]

        inv_f = 1.0 / sf_ref[...]
        for hh in range(n_cols):
            sl = slice(hh * HEAD_DIM, (hh + 1) * HEAD_DIM)
            col = jnp.sum(accf_ref[sl, :], axis=1, keepdims=True) * inv_f[hh:hh + 1, :]
            oft_ref[sl, :] = jnp.where(seq_lane, jnp.broadcast_to(col, (HEAD_DIM, LANES)), oft_ref[sl, :])
        inv_d = 1.0 / sd_ref[...]
        for hh in range(n_dh):
            a = accd_ref[hh] * inv_d
            o = a[2 * hh:2 * hh + 1, :] - lam_ref[0] * a[2 * hh + 1:2 * hh + 2, :]
            od_ref[hh:hh + 1, :] = _rms(o, gs_ref[...]) * out_scale


def _decode_attention(page_table, lam, caches, qft, qdt, new_cols, vdn, slope_col, tri, g_subln,
                      *, pages_per_step, out_scale):
    kf_c, vf_c, lf_c, kd_c, vd_c = caches
    kfn, vfn, lfn, kdn = new_cols
    n_seq, n_pages = page_table.shape
    rows, page = kf_c.shape[1], kf_c.shape[2]
    n_cols = lf_c.shape[1]
    n_dh = vd_c.shape[1] // page
    assert n_seq % LANES == 0 and page == LANES and n_pages % pages_per_step == 0
    pp = pages_per_step
    any_spec = pl.BlockSpec(memory_space=pl.ANY)
    col_spec = lambda r: pl.BlockSpec((r, LANES), lambda s, c, pt: (0, s // LANES))
    const2 = lambda r, c_: pl.BlockSpec((r, c_), lambda s, c, pt: (0, 0))
    grid_spec = pltpu.PrefetchScalarGridSpec(
        num_scalar_prefetch=1,
        grid=(n_seq, n_pages // pp),
        in_specs=[pl.BlockSpec(memory_space=pltpu.SMEM), any_spec, any_spec, any_spec, any_spec, any_spec,
                  col_spec(rows), col_spec(rows), col_spec(rows), col_spec(rows), col_spec(n_cols),
                  col_spec(rows), pl.BlockSpec((None, n_dh, LANES), lambda s, c, pt: (s, 0, 0)),
                  const2(n_cols, LANES), const2(page, page), const2(1, LANES)],
        out_specs=(col_spec(rows), pl.BlockSpec((None, n_dh, LANES), lambda s, c, pt: (s, 0, 0))),
        scratch_shapes=[pltpu.VMEM((2, pp, rows, page), F32), pltpu.VMEM((2, pp, rows, page), F32),
                        pltpu.VMEM((2, pp, n_cols, page), F32), pltpu.VMEM((2, pp, rows, page), F32),
                        pltpu.VMEM((2, pp, page * n_dh, LANES), F32), pltpu.SemaphoreType.DMA((2,)),
                        pltpu.VMEM((rows, LANES), F32), pltpu.VMEM((rows, LANES), F32),
                        pltpu.VMEM((n_cols, 1), F32), pltpu.VMEM((n_cols, 1), F32),
                        pltpu.VMEM((rows, LANES), F32), pltpu.VMEM((n_cols, 1), F32),
                        pltpu.VMEM((n_cols, 1), F32), pltpu.VMEM((n_cols, 1), F32),
                        pltpu.VMEM((n_dh, n_cols, LANES), F32)],
    )
    return pl.pallas_call(
        functools.partial(_decode_kernel, pp=pp, page=page, past=n_pages * page, out_scale=out_scale),
        grid_spec=grid_spec,
        out_shape=(jax.ShapeDtypeStruct((rows, n_seq), F32),
                   jax.ShapeDtypeStruct((n_seq, n_dh, LANES), F32)),
        compiler_params=_cparams(2),
        name="decode_attn",
    )(page_table.reshape(-1), lam, kf_c, vf_c, lf_c, kd_c, vd_c,
      qft, qdt, kfn, vfn, lfn, kdn, vdn, slope_col, tri, g_subln)


def _mix_kernel(o_ref, gate_ref, x_ref, wuf_ref, wud_ref, wo_ref, gn_ref, wr_ref, br_ref, base_ref,
                x1_ref, h2_ref, idx_ref, wgt_ref, rank_ref, cnt_ref, cnt_acc, *, n_experts):
    i = pl.program_id(0)
    tm = x_ref.shape[0]
    half = o_ref.shape[1] // 2
    d = x_ref.shape[1]
    up_f = jnp.dot(o_ref[:, :half], wuf_ref[...], preferred_element_type=F32)
    up_d = jnp.dot(o_ref[:, half:], wud_ref[...], preferred_element_type=F32)
    merged = gate_ref[:, :d].astype(F32) * up_f + gate_ref[:, d:].astype(F32) * up_d
    x1 = x_ref[...] + jnp.dot(merged.astype(BF16), wo_ref[...], preferred_element_type=F32)
    x1_ref[...] = x1
    h2 = _rms(x1, gn_ref[...])
    h2_ref[...] = h2

    logits = jnp.dot(h2, wr_ref[...], precision=HI, preferred_element_type=F32) + br_ref[...]
    lane = lax.broadcasted_iota(I32, logits.shape, 1).astype(F32)
    out_lane = lax.broadcasted_iota(I32, (tm, TOP_K), 1)
    remaining = logits
    picks, vals = [], []
    for _ in range(TOP_K):
        mx = jnp.max(remaining, axis=1, keepdims=True)
        sel = jnp.min(jnp.where(remaining == mx, lane, float(n_experts)), axis=1, keepdims=True)
        hit = lane == sel
        picks.append(hit)
        vals.append(mx)
        remaining = jnp.where(hit, NEG_INF, remaining)
    exps = [jnp.exp(v - vals[0]) for v in vals]
    denom = exps[0] + exps[1] + exps[2] + exps[3]

    @pl.when(i == 0)
    def _():
        cnt_acc[...] = base_ref[...]

    chosen = jnp.zeros(logits.shape, F32)
    for hit in picks:
        chosen = chosen + jnp.where(hit, 1.0, 0.0)
    row = lax.broadcasted_iota(I32, (tm, tm), 0)
    col = lax.broadcasted_iota(I32, (tm, tm), 1)
    strict = jnp.where(col < row, 1.0, 0.0).astype(BF16)
    before = jnp.dot(strict, chosen.astype(BF16), preferred_element_type=F32) + cnt_acc[...]
    cnt_acc[...] = cnt_acc[...] + jnp.sum(chosen, axis=0, keepdims=True)
    cnt_ref[...] = cnt_acc[...]

    idx_out = jnp.zeros((tm, TOP_K), I32)
    wgt_out = jnp.zeros((tm, TOP_K), F32)
    rank_out = jnp.zeros((tm, TOP_K), I32)
    for kk in range(TOP_K):
        sel = jnp.sum(jnp.where(picks[kk], lane, 0.0), axis=1, keepdims=True).astype(I32)
        rk = jnp.sum(jnp.where(picks[kk], before, 0.0), axis=1, keepdims=True).astype(I32)
        idx_out = jnp.where(out_lane == kk, sel, idx_out)
        wgt_out = jnp.where(out_lane == kk, exps[kk] / denom, wgt_out)
        rank_out = jnp.where(out_lane == kk, rk, rank_out)
    idx_ref[...] = idx_out
    wgt_ref[...] = wgt_out
    rank_ref[...] = rank_out


def _mix(o, gates, x, wuf, wud, wo, g_norm, w_router, b_router, base, *, tm):
    m, d = x.shape
    n_exp = w_router.shape[1]
    row_spec = lambda n: pl.BlockSpec((tm, n), lambda i: (i, 0))
    return pl.pallas_call(
        functools.partial(_mix_kernel, n_experts=n_exp),
        grid=(m // tm,),
        in_specs=[row_spec(o.shape[1]), row_spec(gates.shape[1]), row_spec(d),
                  _const_spec(wuf.shape), _const_spec(wud.shape), _const_spec(wo.shape),
                  _const_spec((1, d)), _const_spec(w_router.shape), _const_spec((1, n_exp)),
                  _const_spec((1, n_exp))],
        out_specs=(row_spec(d), row_spec(d), row_spec(TOP_K), row_spec(TOP_K), row_spec(TOP_K),
                   pl.BlockSpec((1, n_exp), lambda i: (0, 0))),
        out_shape=(jax.ShapeDtypeStruct((m, d), F32), jax.ShapeDtypeStruct((m, d), F32),
                   jax.ShapeDtypeStruct((m, TOP_K), I32), jax.ShapeDtypeStruct((m, TOP_K), F32),
                   jax.ShapeDtypeStruct((m, TOP_K), I32), jax.ShapeDtypeStruct((1, n_exp), F32)),
        scratch_shapes=[pltpu.VMEM((1, n_exp), F32)],
        compiler_params=_cparams(1),
        name="mix_router",
    )(o, gates, x, wuf, wud, wo, g_norm, w_router, b_router, base)


def _row_copy(src, src_row, dst, dst_row, sem):
    return pltpu.make_async_copy(src.at[pl.ds(src_row, 1), :], dst.at[pl.ds(dst_row, 1), :], sem)


def _dispatch_kernel(dest_ref, h_ref, rows_in_ref, rows_ref, sem, *, tc):
    del rows_in_ref

    def issue(t, carry):
        for kk in range(TOP_K):
            _row_copy(h_ref, t, rows_ref, dest_ref[t * TOP_K + kk], sem).start()
        return carry

    lax.fori_loop(0, tc, issue, 0, unroll=8)
    for kk in range(TOP_K):
        pltpu.make_async_copy(h_ref, rows_ref.at[pl.ds(0, tc), :], sem).wait()


def _dispatch(dest_flat, h, rows, *, tc):
    m, d = h.shape
    return pl.pallas_call(
        functools.partial(_dispatch_kernel, tc=tc),
        grid=(m // tc,),
        in_specs=[pl.BlockSpec((tc * TOP_K,), lambda i: (i,), memory_space=pltpu.SMEM),
                  pl.BlockSpec((tc, d), lambda i: (i, 0)), pl.BlockSpec(memory_space=pl.ANY)],
        out_specs=pl.BlockSpec(memory_space=pl.ANY),
        out_shape=jax.ShapeDtypeStruct(rows.shape, rows.dtype),
        scratch_shapes=[pltpu.SemaphoreType.DMA(())],
        input_output_aliases={2: 0},
        compiler_params=_cparams(1, has_side_effects=True),
        name="dispatch",
    )(dest_flat, h, rows)


def _combine_kernel(dest_ref, y_ref, w_ref, x1_ref, g_ref, o_ref, buf, sem, *, tc, final_norm):
    def issue(t, carry):
        for kk in range(TOP_K):
            _row_copy(y_ref, dest_ref[t * TOP_K + kk], buf.at[kk], t, sem).start()
        return carry

    lax.fori_loop(0, tc, issue, 0, unroll=8)
    for kk in range(TOP_K):
        pltpu.make_async_copy(y_ref.at[pl.ds(0, tc), :], buf.at[kk], sem).wait()
    out = x1_ref[...]
    for kk in range(TOP_K):
        out = out + buf[kk] * w_ref[:, kk:kk + 1]
    if final_norm:
        out = _rms(out, g_ref[...])
    o_ref[...] = out


def _combine(dest_flat, y_rows, wgt, x1, g, *, tc, final_norm):
    m, d = x1.shape
    return pl.pallas_call(
        functools.partial(_combine_kernel, tc=tc, final_norm=final_norm),
        grid=(m // tc,),
        in_specs=[pl.BlockSpec((tc * TOP_K,), lambda i: (i,), memory_space=pltpu.SMEM),
                  pl.BlockSpec(memory_space=pl.ANY),
                  pl.BlockSpec((tc, TOP_K), lambda i: (i, 0)),
                  pl.BlockSpec((tc, d), lambda i: (i, 0)),
                  pl.BlockSpec((1, d), lambda i: (0, 0))],
        out_specs=pl.BlockSpec((tc, d), lambda i: (i, 0)),
        out_shape=jax.ShapeDtypeStruct((m, d), F32),
        scratch_shapes=[pltpu.VMEM((TOP_K, tc, d), F32), pltpu.SemaphoreType.DMA(())],
        compiler_params=_cparams(1),
        name="combine",
    )(dest_flat, y_rows, wgt, x1, g)


def _expert_kernel(be_ref, na_ref, x_ref, wgu_ref, bgu_ref, wd_ref, bd_ref, y_ref, wgu_bf, wd_bf):
    i = pl.program_id(0)
    d_ff = wd_ref.shape[0]
    active = i < na_ref[0]
    new_expert = jnp.logical_or(i == 0, be_ref[i] != be_ref[jnp.maximum(i - 1, 0)])

    @pl.when(jnp.logical_and(active, new_expert))
    def _():
        wgu_bf[...] = wgu_ref[...].astype(BF16)
        wd_bf[...] = wd_ref[...].astype(BF16)

    @pl.when(active)
    def _():
        gu = jnp.dot(x_ref[...].astype(BF16), wgu_bf[...], preferred_element_type=F32) + bgu_ref[...]
        glu = jnp.minimum(gu[:, :d_ff], SWIGLU_LIMIT)
        lin = jnp.clip(gu[:, d_ff:], -SWIGLU_LIMIT, SWIGLU_LIMIT)
        act = glu * (1.0 / (1.0 + jnp.exp(-SWIGLU_ALPHA * glu))) * (lin + 1.0)
        y_ref[...] = jnp.dot(act.astype(BF16), wd_bf[...], preferred_element_type=F32) + bd_ref[...]

    @pl.when(i >= na_ref[0])
    def _():
        y_ref[...] = jnp.zeros_like(y_ref)


def _experts(block_expert, n_active, x_rows, wgu, bgu, wd, bd):
    n_rows, d = x_rows.shape
    n_blocks = n_rows // ROW_BLOCK
    grid_spec = pltpu.PrefetchScalarGridSpec(
        num_scalar_prefetch=2,
        grid=(n_blocks,),
        in_specs=[pl.BlockSpec((ROW_BLOCK, d), lambda i, be, na: (i, 0)),
                  pl.BlockSpec((None,) + wgu.shape[1:], lambda i, be, na: (be[i], 0, 0)),
                  pl.BlockSpec((None,) + bgu.shape[1:], lambda i, be, na: (be[i], 0, 0)),
                  pl.BlockSpec((None,) + wd.shape[1:], lambda i, be, na: (be[i], 0, 0)),
                  pl.BlockSpec((None,) + bd.shape[1:], lambda i, be, na: (be[i], 0, 0))],
        out_specs=pl.BlockSpec((ROW_BLOCK, d), lambda i, be, na: (i, 0)),
        scratch_shapes=[pltpu.VMEM(wgu.shape[1:], BF16), pltpu.VMEM(wd.shape[1:], BF16)],
    )
    return pl.pallas_call(
        _expert_kernel,
        grid_spec=grid_spec,
        out_shape=jax.ShapeDtypeStruct((n_rows, d), F32),
        compiler_params=_cparams(1),
        name="experts",
    )(block_expert, n_active, x_rows, wgu, bgu, wd, bd)


def kernel(x_prompt, x_sample, cache_fox_k, cache_fox_v, cache_fox_logf, cache_diff_k, cache_diff_v,
           page_table, g_attn_norm, w_in, b_forget, b_gate, lambda_q1, lambda_k1, lambda_q2, lambda_k2,
           g_subln, w_up_fox, w_up_diff, w_out, g_ffn_norm, w_router, b_router, w_gate_up, b_gate_up,
           w_down, b_down, g_final):
    depth = w_in.shape[0]
    b, t, d = x_prompt.shape
    n_seq = x_sample.shape[0]
    assert x_sample.shape[1] == 1
    n_phys, page, n_fox = cache_fox_k.shape[1:4]
    n_diff = cache_diff_k.shape[3]
    width = n_fox * HEAD_DIM
    assert 2 * n_diff == n_fox and cache_diff_v.shape[4] == 2 * HEAD_DIM
    n_exp = w_router.shape[2]
    m_p = b * t
    slopes = 2.0 ** (-8.0 * jnp.arange(1, n_diff + 1, dtype=F32) / n_diff)
    slope_l2 = (jnp.repeat(slopes, 2) * LOG2E)
    slope_col = jnp.broadcast_to(slope_l2[:, None], (n_fox, LANES))
    pq, pk = _bias_placement(2 * n_fox)
    tri_u = (jnp.arange(page)[:, None] <= jnp.arange(page)[None, :]).astype(BF16)
    tm_p = min(TOKEN_BLOCK, t)

    xp = x_prompt.reshape(m_p, d)
    xs = x_sample.reshape(n_seq, d)
    outs_p, outs_s = [], []
    for l in range(depth):
        lam_init = 0.8 - 0.6 * math.exp(-0.3 * l)
        lam = (jnp.exp(jnp.sum(lambda_q1[l].astype(F32) * lambda_k1[l].astype(F32)))
               - jnp.exp(jnp.sum(lambda_q2[l].astype(F32) * lambda_k2[l].astype(F32)))
               + lam_init).reshape(1)
        out_scale = 1.0 - lam_init
        w = w_in[l]
        o_fq, o_fk, o_fv, o_zf = 0, width, 2 * width, 3 * width
        o_dq = o_zf + n_fox
        o_dk, o_dv, o_zg = o_dq + width, o_dq + 2 * width, o_dq + 3 * width
        sl = lambda o, n=width: w[:, o:o + n]
        w_nn = jnp.concatenate([sl(o_fq), sl(o_dq), sl(o_fk), sl(o_dk), sl(o_dv), w[:, o_zg:]],
                               axis=1).astype(BF16)
        w_nt = jnp.concatenate([sl(o_fk), sl(o_dk), sl(o_fv), sl(o_dv)], axis=1).T.astype(BF16)
        w_f = sl(o_zf, n_fox)
        g_attn = g_attn_norm[l][None]
        b_f = b_forget[l][None]
        b_g = b_gate[l][None]
        gs = g_subln[l][None]
        wuf, wud, wo = (w_up_fox[l].astype(BF16), w_up_diff[l].astype(BF16), w_out[l].astype(BF16))
        g_ffn = g_ffn_norm[l][None]
        last = l == depth - 1
        g_out = g_final[None] if last else jnp.ones((1, d), F32)
        proj = functools.partial(_proj, g=g_attn, w_nn=w_nn, w_nt=w_nt, w_f=w_f, b_f=b_f, b_g=b_g,
                                 slope_l2=slope_l2[None], pq=pq, pk=pk)

        (q_p, qb_p, k_p, kb_p, vt_p, fkt_p, fvt_p, dkt_p, dv_p, lft_p, gate_p) = proj(
            xp, tm=tm_p, n_batch=b)
        r3 = lambda a: a.reshape(b, t, a.shape[-1])
        o_p = _prompt_attention(lam, r3(q_p), r3(qb_p), r3(k_p), r3(kb_p), vt_p, gs,
                                blk=tm_p, n_fox_units=n_fox // 2, out_scale=out_scale)
        x1_p, h2_p, idx_p, wgt_p, rank_p, cnt_p = _mix(
            o_p.reshape(m_p, 2 * width), gate_p, xp, wuf, wud, wo, g_ffn, w_router[l], b_router[l][None],
            jnp.zeros((1, n_exp), F32), tm=tm_p)

        (q_s, _, _, _, _, fkt_s, fvt_s, dkt_s, dv_s, lft_s, gate_s) = proj(xs, tm=n_seq, n_batch=1)
        caches = (jnp.transpose(cache_fox_k[l], (0, 2, 3, 1)).reshape(n_phys, width, page),
                  jnp.transpose(cache_fox_v[l], (0, 2, 3, 1)).reshape(n_phys, width, page),
                  jnp.transpose(cache_fox_logf[l], (0, 2, 1)),
                  jnp.transpose(cache_diff_k[l], (0, 2, 3, 4, 1)).reshape(n_phys, width, page),
                  cache_diff_v[l].reshape(n_phys, page * n_diff, 2 * HEAD_DIM))
        q_t = q_s.astype(F32).T
        oft, od = _decode_attention(
            page_table, lam, caches, q_t[:width], q_t[width:],
            (fkt_s[0], fvt_s[0], lft_s[0], dkt_s[0]), dv_s.reshape(n_seq, n_diff, 2 * HEAD_DIM),
            slope_col, tri_u, gs, pages_per_step=DECODE_PAGES, out_scale=out_scale)
        o_s = jnp.concatenate([oft.T, od.reshape(n_seq, width)], axis=1).astype(BF16)
        x1_s, h2_s, idx_s, wgt_s, rank_s, cnt_all = _mix(
            o_s, gate_s, xs, wuf, wud, wo, g_ffn, w_router[l], b_router[l][None], cnt_p, tm=n_seq)

        counts = cnt_all[0].astype(I32)
        padded = (counts + ROW_BLOCK - 1) // ROW_BLOCK * ROW_BLOCK
        pad_end = jnp.cumsum(padded)
        pad_start = pad_end - padded
        n_assign = (m_p + n_seq) * TOP_K
        n_blocks = -(-n_assign // ROW_BLOCK) + n_exp
        block_start = jnp.arange(n_blocks, dtype=I32) * ROW_BLOCK
        block_expert = jnp.minimum(jnp.sum((pad_end[None, :] <= block_start[:, None]).astype(I32), axis=1),
                                   n_exp - 1).astype(I32)
        n_active = (pad_end[-1:] // ROW_BLOCK).astype(I32)
        dest_p = (pad_start[idx_p] + rank_p).reshape(-1)
        dest_s = (pad_start[idx_s] + rank_s).reshape(-1)

        x_rows = jnp.zeros((n_blocks * ROW_BLOCK, d), F32)
        x_rows = _dispatch(dest_p, h2_p, x_rows, tc=2 * ROUTE_BLOCK)
        x_rows = _dispatch(dest_s, h2_s, x_rows, tc=ROUTE_BLOCK)
        y_rows = _experts(block_expert, n_active, x_rows, w_gate_up[l], b_gate_up[l][:, None],
                          w_down[l], b_down[l][:, None])
        xp = _combine(dest_p, y_rows, wgt_p, x1_p, g_out, tc=ROUTE_BLOCK, final_norm=last)
        xs = _combine(dest_s, y_rows, wgt_s, x1_s, g_out, tc=ROUTE_BLOCK, final_norm=last)

        def kv_rows(kt_, n_b, n_t):
            return jnp.transpose(kt_.reshape(n_b, -1, HEAD_DIM, n_t), (0, 3, 1, 2))

        def cache_rows(fkt_, fvt_, lft_, dkt_, dv_, n_b, n_t):
            return (kv_rows(fkt_, n_b, n_t), kv_rows(fvt_, n_b, n_t), jnp.transpose(lft_, (0, 2, 1)),
                    kv_rows(dkt_, n_b, n_t).reshape(n_b, n_t, n_diff, 2, HEAD_DIM),
                    dv_.reshape(n_b, n_t, n_diff, 2 * HEAD_DIM))

        outs_p.append(cache_rows(fkt_p, fvt_p, lft_p, dkt_p, dv_p, b, t))
        rows_s = cache_rows(fkt_s, fvt_s, lft_s, dkt_s, dv_s, 1, n_seq)
        outs_s.append(tuple(a.reshape((n_seq, 1) + a.shape[2:]) for a in rows_s))

    stack = lambda rows: [jnp.stack(r) for r in zip(*rows)]
    return (xp.reshape(b, t, d), xs.reshape(n_seq, 1, d), *stack(outs_p), *stack(outs_s))
```

```python
import functools
import math

import numpy as np
import jax
import jax.numpy as jnp
from jax import lax
from jax.experimental import pallas as pl
from jax.experimental.pallas import tpu as pltpu

F32 = jnp.float32
BF16 = jnp.bfloat16
I32 = jnp.int32

HEAD_DIM = 64
LANES = 128
EPS = 1e-5
SWIGLU_LIMIT = 7.0
SWIGLU_ALPHA = 1.702
TOP_K = 4
ROW_BLOCK = 256
ROUTE_BLOCK = 128
TOKEN_BLOCK = 512
DECODE_PAGES = 16
VMEM_LIMIT = 56 * 1024 * 1024
NEG_INF = float("-inf")
LOG2E = math.log2(math.e)
HI = lax.Precision.HIGHEST
NT_DIMS = (((1,), (1,)), ((), ()))


def _cparams(n_axes, **kw):
    return pltpu.CompilerParams(dimension_semantics=("arbitrary",) * n_axes,
                                vmem_limit_bytes=VMEM_LIMIT, **kw)


def _const_spec(shape):
    nd = len(shape)
    return pl.BlockSpec(shape, lambda *_: (0,) * nd, pipeline_mode=pl.Buffered(1))


def _trunc_bf16(x):
    bits = pltpu.bitcast(x, jnp.uint32) & jnp.uint32(0xFFFF0000)
    return pltpu.bitcast(bits, F32)


def _split3(x):
    hi = _trunc_bf16(x)
    r = x - hi
    mid = _trunc_bf16(r)
    return hi, mid, r - mid


def _rms(x, g):
    ms = jnp.mean(x * x, axis=-1, keepdims=True)
    return x * lax.rsqrt(ms + EPS) * g


def _proj_kernel(x_ref, g_ref, w_ref, wt_ref, wf_ref, wft_ref, bf_ref, bft_ref, bg_ref, slope_ref,
                 pq_ref, pk_ref,
                 q_ref, qb_ref, k_ref, kb_ref, vt_ref, fkt_ref, fvt_ref, dkt_ref, dv_ref, lft_ref, gate_ref,
                 carry_ref, *, blocks_per_seq, width):
    i = pl.program_id(0)
    tm = x_ref.shape[0]
    h = _rms(x_ref[...], g_ref[...])
    hb = h.astype(BF16)
    scale = HEAD_DIM ** -0.5 * LOG2E

    def proj(col):
        return jnp.dot(hb, w_ref[:, col * width:(col + 1) * width], preferred_element_type=F32)

    def proj_t(grp):
        return lax.dot_general(wt_ref[grp * width:(grp + 1) * width, :], hb, NT_DIMS,
                               preferred_element_type=F32)

    q_ref[:, :width] = (proj(0) * scale).astype(BF16)
    q_ref[:, width:] = (proj(1) * scale).astype(BF16)
    k_ref[:, :width] = proj(2).astype(BF16)
    k_ref[:, width:] = proj(3).astype(BF16)
    dv_ref[...] = proj(4)
    fkt_ref[...] = proj_t(0)
    dkt_ref[...] = proj_t(1)
    fvt = proj_t(2)
    fvt_ref[...] = fvt
    vt_ref[:width, :] = fvt.astype(BF16)
    vt_ref[width:, :] = proj_t(3).astype(BF16)
    n_gate = gate_ref.shape[1] // width
    for j in range(n_gate):
        z = proj(5 + j) + bg_ref[:, j * width:(j + 1) * width]
        gate_ref[:, j * width:(j + 1) * width] = (1.0 / (1.0 + jnp.exp(-z))).astype(BF16)

    def log_sigmoid(z):
        return jnp.minimum(z, 0.0) - jnp.log1p(jnp.exp(-jnp.abs(z)))

    zft = lax.dot_general(wft_ref[...], h, NT_DIMS, precision=HI, preferred_element_type=F32)
    lft_ref[...] = log_sigmoid(zft + bft_ref[...])
    zf = jnp.dot(h, wf_ref[...], precision=HI, preferred_element_type=F32)
    lf = log_sigmoid(zf + bf_ref[...])

    @pl.when(i % blocks_per_seq == 0)
    def _():
        carry_ref[...] = jnp.zeros_like(carry_ref)

    row = lax.broadcasted_iota(I32, (tm, tm), 0)
    col = lax.broadcasted_iota(I32, (tm, tm), 1)
    tri = jnp.where(col <= row, 1.0, 0.0).astype(BF16)
    c = carry_ref[...]
    for part in _split3(lf):
        c = c + jnp.dot(tri, part.astype(BF16), preferred_element_type=F32)
    carry_ref[...] = c[tm - 1:tm, :]

    pos = ((i % blocks_per_seq) * tm + lax.broadcasted_iota(I32, (tm, 1), 0)).astype(F32)
    a = jnp.concatenate([c * LOG2E, -(slope_ref[...] * pos)], axis=1)
    hi, mid, lo = _split3(a)
    pieces = jnp.concatenate([hi, mid, lo, jnp.ones_like(a)], axis=1).astype(BF16)
    qb_ref[...] = jnp.dot(pieces, pq_ref[...], preferred_element_type=F32).astype(BF16)
    kb_ref[...] = jnp.dot(pieces, pk_ref[...], preferred_element_type=F32).astype(BF16)


def _proj(x, g, w_nn, w_nt, w_f, b_f, b_g, slope_l2, pq, pk, *, tm, n_batch):
    m, d = x.shape
    width = w_nt.shape[0] // 4
    n_f = w_f.shape[1]
    gate_w = b_g.shape[1]
    t = m // n_batch
    bps = t // tm
    row_spec = lambda n: pl.BlockSpec((tm, n), lambda i: (i, 0))
    tr_spec = lambda r: pl.BlockSpec((None, r, tm), lambda i: (i // bps, 0, i % bps))
    out_shape = (
        jax.ShapeDtypeStruct((m, 2 * width), BF16),
        jax.ShapeDtypeStruct((m, 2 * width), BF16),
        jax.ShapeDtypeStruct((m, 2 * width), BF16),
        jax.ShapeDtypeStruct((m, 2 * width), BF16),
        jax.ShapeDtypeStruct((m // tm, 2 * width, tm), BF16),
        jax.ShapeDtypeStruct((n_batch, width, t), F32),
        jax.ShapeDtypeStruct((n_batch, width, t), F32),
        jax.ShapeDtypeStruct((n_batch, width, t), F32),
        jax.ShapeDtypeStruct((m, width), F32),
        jax.ShapeDtypeStruct((n_batch, n_f, t), F32),
        jax.ShapeDtypeStruct((m, gate_w), BF16),
    )
    out_specs = (row_spec(2 * width), row_spec(2 * width), row_spec(2 * width), row_spec(2 * width),
                 pl.BlockSpec((None, 2 * width, tm), lambda i: (i, 0, 0)),
                 tr_spec(width), tr_spec(width), tr_spec(width), row_spec(width), tr_spec(n_f),
                 row_spec(gate_w))
    consts = (g, w_nn, w_nt, w_f, w_f.T, b_f, b_f.T, b_g, slope_l2, pq, pk)
    return pl.pallas_call(
        functools.partial(_proj_kernel, blocks_per_seq=bps, width=width),
        grid=(m // tm,),
        in_specs=[row_spec(d)] + [_const_spec(c.shape) for c in consts],
        out_specs=out_specs,
        out_shape=out_shape,
        scratch_shapes=[pltpu.VMEM((1, n_f), F32)],
        compiler_params=_cparams(1),
        name="proj",
    )(x, *consts)


def _bias_placement(n_cols):
    pq = np.zeros((4 * n_cols, n_cols * HEAD_DIM), np.float32)
    pk = np.zeros_like(pq)
    for j in range(n_cols):
        base = j * HEAD_DIM
        for part in range(3):
            pq[part * n_cols + j, base + part] = 1.0
            pq[3 * n_cols + j, base + 3 + part] = 1.0
            pk[3 * n_cols + j, base + part] = 1.0
            pk[part * n_cols + j, base + 3 + part] = -1.0
    return jnp.asarray(pq, BF16), jnp.asarray(pk, BF16)


def _attn_kernel(lam_ref, q_ref, qb_ref, k_ref, kb_ref, vt_ref, gs_ref, o_ref,
                 acc_ref, m_ref, s0_ref, s1_ref, *, blk, n_fox_units, out_scale):
    u = pl.program_id(1)
    qi = pl.program_id(2)
    q2 = jnp.concatenate([q_ref[...], qb_ref[...]], axis=1)
    lane2 = lax.broadcasted_iota(I32, q2.shape, 1)
    first_half = (lane2 & (LANES - 1)) < HEAD_DIM
    zero = jnp.zeros_like(q2)
    q_both = jnp.concatenate([jnp.where(first_half, q2, zero), jnp.where(first_half, zero, q2)], axis=0)
    ones_rows = jnp.ones((16, blk), BF16)

    m_ref[...] = jnp.full_like(m_ref, NEG_INF)
    acc_ref[...] = jnp.zeros_like(acc_ref)

    def scores(kj, st_ref):
        off = pl.multiple_of(kj * blk, blk)
        k2 = jnp.concatenate([k_ref[pl.ds(off, blk), :], kb_ref[pl.ds(off, blk), :]], axis=1)
        st_ref[...] = lax.dot_general(k2, q_both, NT_DIMS, preferred_element_type=F32)

    def accumulate(kj, st_ref, masked):
        vt = jnp.concatenate([vt_ref[kj], ones_rows], axis=0)
        st = st_ref[...]
        if masked:
            krow = lax.broadcasted_iota(I32, st.shape, 0)
            qcol = lax.broadcasted_iota(I32, st.shape, 1) & (blk - 1)
            st = jnp.where(krow <= qcol, st, NEG_INF)
        m_old = m_ref[...]
        m_new = jnp.maximum(m_old, jnp.max(st, axis=0, keepdims=True))
        alpha = jnp.exp2(m_old - m_new)
        p = jnp.exp2((st - m_new).astype(BF16))
        acc_ref[...] = acc_ref[...] * alpha + jnp.dot(vt, p, preferred_element_type=F32)
        m_ref[...] = m_new

    scores(0, s0_ref)

    def pair(jj, carry):
        scores(2 * jj + 1, s1_ref)
        accumulate(2 * jj, s0_ref, False)
        scores(2 * jj + 2, s0_ref)
        accumulate(2 * jj + 1, s1_ref, False)
        return carry

    lax.fori_loop(0, qi // 2, pair, 0)

    @pl.when(qi % 2 == 0)
    def _():
        accumulate(qi, s0_ref, True)

    @pl.when(qi % 2 == 1)
    def _():
        scores(qi, s1_ref)
        accumulate(qi - 1, s0_ref, False)
        accumulate(qi, s1_ref, True)

    acc = acc_ref[...]
    o_a = acc[:LANES, :blk] / acc[LANES:LANES + 1, :blk]
    o_b = acc[:LANES, blk:] / acc[LANES:LANES + 1, blk:]

    @pl.when(u < n_fox_units)
    def _():
        r = lax.broadcasted_iota(I32, o_a.shape, 0)
        o_ref[...] = jnp.where(r < HEAD_DIM, o_a, o_b).T.astype(BF16)

    @pl.when(u >= n_fox_units)
    def _():
        o = (o_a - lam_ref[0] * o_b).T
        o_ref[...] = (_rms(o, gs_ref[...]) * out_scale).astype(BF16)


def _prompt_attention(lam, q, qb, k, kb, vt, g_subln, *, blk, n_fox_units, out_scale):
    b, t, w = q.shape
    n_units = w // LANES
    nkb = t // blk
    q_spec = pl.BlockSpec((None, blk, LANES), lambda bi, u, qi: (bi, qi, u))
    k_spec = pl.BlockSpec((None, t, LANES), lambda bi, u, qi: (bi, 0, u))
    vt_spec = pl.BlockSpec((nkb, LANES, blk), lambda bi, u, qi: (bi, u, 0))
    return pl.pallas_call(
        functools.partial(_attn_kernel, blk=blk, n_fox_units=n_fox_units, out_scale=out_scale),
        grid=(b, n_units, nkb),
        in_specs=[pl.BlockSpec(memory_space=pltpu.SMEM), q_spec, q_spec, k_spec, k_spec, vt_spec,
                  pl.BlockSpec((1, LANES), lambda bi, u, qi: (0, 0))],
        out_specs=q_spec,
        out_shape=jax.ShapeDtypeStruct((b, t, w), BF16),
        scratch_shapes=[pltpu.VMEM((LANES + 16, 2 * blk), F32), pltpu.VMEM((1, 2 * blk), F32),
                        pltpu.VMEM((blk, 2 * blk), F32), pltpu.VMEM((blk, 2 * blk), F32)],
        compiler_params=_cparams(3),
        name="prompt_attn",
    )(lam, q, qb, k, kb, vt, g_subln)


def _decode_kernel(pt_ref, lam_ref, kf_hbm, vf_hbm, lf_hbm, kd_hbm, vd_hbm,
                   qft_ref, qdt_ref, kfn_ref, vfn_ref, lfn_ref, kdn_ref, vdn_ref, slope_ref, tri_ref, gs_ref,
                   oft_ref, od_ref,
                   kf_buf, vf_buf, lf_buf, kd_buf, vd_buf, sem,
                   qf_col, qd_col, mf_ref, sf_ref, accf_ref, cpre_ref, md_ref, sd_ref, accd_ref,
                   *, pp, page, past, out_scale):
    s_idx = pl.program_id(0)
    c = pl.program_id(1)
    n_chunks = pl.num_programs(1)
    n = s_idx * n_chunks + c
    total = pl.num_programs(0) * n_chunks
    slot = n % 2
    n_cols = mf_ref.shape[0]
    rows = kf_buf.shape[2]
    n_dh = accd_ref.shape[0]
    lane = lax.broadcasted_iota(I32, (1, LANES), 1)
    seq_lane = lane == s_idx % LANES

    def chunk_copies(chunk, slot_):
        cps = []
        for j in range(pp):
            pg = pt_ref[chunk * pp + j]
            for hbm, buf in ((kf_hbm, kf_buf), (vf_hbm, vf_buf), (lf_hbm, lf_buf), (kd_hbm, kd_buf),
                             (vd_hbm, vd_buf)):
                cps.append(pltpu.make_async_copy(hbm.at[pg], buf.at[slot_, j], sem.at[slot_]))
        return cps

    @pl.when(n == 0)
    def _():
        for cp in chunk_copies(0, 0):
            cp.start()

    @pl.when(n + 1 < total)
    def _():
        for cp in chunk_copies(n + 1, 1 - slot):
            cp.start()

    def pick(ref):
        col = jnp.sum(jnp.where(seq_lane, ref[...], 0.0), axis=1, keepdims=True)
        return jnp.broadcast_to(col, (ref.shape[0], LANES))

    @pl.when(c == 0)
    def _():
        qf_col[...] = pick(qft_ref)
        qd_col[...] = pick(qdt_ref)
        mf_ref[...] = jnp.full_like(mf_ref, NEG_INF)
        md_ref[...] = jnp.full_like(md_ref, NEG_INF)
        sf_ref[...] = jnp.zeros_like(sf_ref)
        sd_ref[...] = jnp.zeros_like(sd_ref)
        accf_ref[...] = jnp.zeros_like(accf_ref)
        accd_ref[...] = jnp.zeros_like(accd_ref)
        cpre_ref[...] = jnp.zeros_like(cpre_ref)

    @pl.when(jnp.logical_and(c == 0, s_idx % LANES == 0))
    def _():
        oft_ref[...] = jnp.zeros_like(oft_ref)

    def scores(kt, qcol_ref):
        out = []
        for hh in range(n_cols):
            sl = slice(hh * HEAD_DIM, (hh + 1) * HEAD_DIM)
            out.append(jnp.sum(kt[sl, :] * qcol_ref[sl, :], axis=0, keepdims=True))
        return jnp.concatenate(out, axis=0)

    def softmax_step(s, m_ref, s_ref):
        m_old = m_ref[...]
        m_new = jnp.maximum(m_old, jnp.max(s, axis=1, keepdims=True))
        alpha = jnp.exp2(m_old - m_new)
        p = jnp.exp2(s - m_new)
        s_ref[...] = alpha * s_ref[...] + jnp.sum(p, axis=1, keepdims=True)
        m_ref[...] = m_new
        return alpha, p

    def fox_update(s, vts):
        alpha, p = softmax_step(s, mf_ref, sf_ref)
        for hh in range(n_cols):
            sl = slice(hh * HEAD_DIM, (hh + 1) * HEAD_DIM)
            acc = accf_ref[sl, :] * alpha[hh:hh + 1, :]
            for j, vt in enumerate(vts):
                acc = acc + vt[sl, :] * p[hh:hh + 1, j * LANES:(j + 1) * LANES]
            accf_ref[sl, :] = acc

    first_lane = lane == 0

    for cp in chunk_copies(n, slot):
        cp.wait()

    s_pages = []
    cpre = cpre_ref[...]
    for j in range(pp):
        s = scores(kf_buf[slot, j], qf_col)
        lf = lf_buf[slot, j]
        hi, mid, lo = _split3(lf)
        parts = jnp.concatenate([hi, mid, lo, jnp.zeros_like(lf)], axis=0).astype(BF16)
        r = jnp.dot(parts, tri_ref[...], preferred_element_type=F32)
        cs = r[0:n_cols] + r[n_cols:2 * n_cols] + r[2 * n_cols:3 * n_cols] + cpre
        cpre = cpre + jnp.sum(lf, axis=1, keepdims=True)
        s_pages.append(s - cs * LOG2E)
    cpre_ref[...] = cpre
    fox_update(jnp.concatenate(s_pages, axis=1), [vf_buf[slot, j] for j in range(pp)])

    sd_pages = []
    for j in range(pp):
        dist = (past - ((c * pp + j) * page + lane)).astype(F32)
        sd_pages.append(scores(kd_buf[slot, j], qd_col) - slope_ref[...] * dist)
    alpha, p = softmax_step(jnp.concatenate(sd_pages, axis=1), md_ref, sd_ref)
    pb = p.astype(BF16)
    for hh in range(n_dh):
        acc = accd_ref[hh] * alpha
        for j in range(pp):
            vh = vd_buf[slot, j, pl.ds(hh, page, stride=n_dh), :].astype(BF16)
            acc = acc + jnp.dot(pb[:, j * LANES:(j + 1) * LANES], vh, preferred_element_type=F32)
        accd_ref[hh] = acc

    @pl.when(c == n_chunks - 1)
    def _():
        s_new = scores(pick(kfn_ref), qf_col)
        c_tot = cpre_ref[...] + pick(lfn_ref)
        s_new = jnp.where(first_lane, s_new - c_tot * LOG2E, NEG_INF)
        fox_update(s_new, [pick(vfn_ref)])
        sd_new = jnp.where(first_lane, scores(pick(kdn_ref), qd_col), NEG_INF)
        alpha, p = softmax_step(sd_new, md_ref, sd_ref)
        p_new = jnp.sum(p, axis=1, keepdims=True)
        for hh in range(n_dh):
            accd_ref[hh] = accd_ref[hh] * alpha + p_new * vdn_ref[hh:hh + 1, :]

        inv_f = 1.0 / sf_ref[...]
        for hh in range(n_cols):
            sl = slice(hh * HEAD_DIM, (hh + 1) * HEAD_DIM)
            col = jnp.sum(accf_ref[sl, :], axis=1, keepdims=True) * inv_f[hh:hh + 1, :]
            oft_ref[sl, :] = jnp.where(seq_lane, jnp.broadcast_to(col, (HEAD_DIM, LANES)), oft_ref[sl, :])
        inv_d = 1.0 / sd_ref[...]
        for hh in range(n_dh):
            a = accd_ref[hh] * inv_d
            o = a[2 * hh:2 * hh + 1, :] - lam_ref[0] * a[2 * hh + 1:2 * hh + 2, :]
            od_ref[hh:hh + 1, :] = _rms(o, gs_ref[...]) * out_scale


def _decode_attention(page_table, lam, caches, qft, qdt, new_cols, vdn, slope_col, tri, g_subln,
                      *, pages_per_step, out_scale):
    kf_c, vf_c, lf_c, kd_c, vd_c = caches
    kfn, vfn, lfn, kdn = new_cols
    n_seq, n_pages = page_table.shape
    rows, page = kf_c.shape[1], kf_c.shape[2]
    n_cols = lf_c.shape[1]
    n_dh = vd_c.shape[1] // page
    assert n_seq % LANES == 0 and page == LANES and n_pages % pages_per_step == 0
    pp = pages_per_step
    any_spec = pl.BlockSpec(memory_space=pl.ANY)
    col_spec = lambda r: pl.BlockSpec((r, LANES), lambda s, c, pt: (0, s // LANES))
    const2 = lambda r, c_: pl.BlockSpec((r, c_), lambda s, c, pt: (0, 0))
    grid_spec = pltpu.PrefetchScalarGridSpec(
        num_scalar_prefetch=1,
        grid=(n_seq, n_pages // pp),
        in_specs=[pl.BlockSpec(memory_space=pltpu.SMEM), any_spec, any_spec, any_spec, any_spec, any_spec,
                  col_spec(rows), col_spec(rows), col_spec(rows), col_spec(rows), col_spec(n_cols),
                  col_spec(rows), pl.BlockSpec((None, n_dh, LANES), lambda s, c, pt: (s, 0, 0)),
                  const2(n_cols, LANES), const2(page, page), const2(1, LANES)],
        out_specs=(col_spec(rows), pl.BlockSpec((None, n_dh, LANES), lambda s, c, pt: (s, 0, 0))),
        scratch_shapes=[pltpu.VMEM((2, pp, rows, page), F32), pltpu.VMEM((2, pp, rows, page), F32),
                        pltpu.VMEM((2, pp, n_cols, page), F32), pltpu.VMEM((2, pp, rows, page), F32),
                        pltpu.VMEM((2, pp, page * n_dh, LANES), F32), pltpu.SemaphoreType.DMA((2,)),
                        pltpu.VMEM((rows, LANES), F32), pltpu.VMEM((rows, LANES), F32),
                        pltpu.VMEM((n_cols, 1), F32), pltpu.VMEM((n_cols, 1), F32),
                        pltpu.VMEM((rows, LANES), F32), pltpu.VMEM((n_cols, 1), F32),
                        pltpu.VMEM((n_cols, 1), F32), pltpu.VMEM((n_cols, 1), F32),
                        pltpu.VMEM((n_dh, n_cols, LANES), F32)],
    )
    return pl.pallas_call(
        functools.partial(_decode_kernel, pp=pp, page=page, past=n_pages * page, out_scale=out_scale),
        grid_spec=grid_spec,
        out_shape=(jax.ShapeDtypeStruct((rows, n_seq), F32),
                   jax.ShapeDtypeStruct((n_seq, n_dh, LANES), F32)),
        compiler_params=_cparams(2),
        name="decode_attn",
    )(page_table.reshape(-1), lam, kf_c, vf_c, lf_c, kd_c, vd_c,
      qft, qdt, kfn, vfn, lfn, kdn, vdn, slope_col, tri, g_subln)


def _mix_kernel(o_ref, gate_ref, x_ref, wuf_ref, wud_ref, wo_ref, gn_ref, wr_ref, br_ref, base_ref,
                x1_ref, h2_ref, idx_ref, wgt_ref, rank_ref, cnt_ref, cnt_acc, *, n_experts):
    i = pl.program_id(0)
    tm = x_ref.shape[0]
    half = o_ref.shape[1] // 2
    d = x_ref.shape[1]
    up_f = jnp.dot(o_ref[:, :half], wuf_ref[...], preferred_element_type=F32)
    up_d = jnp.dot(o_ref[:, half:], wud_ref[...], preferred_element_type=F32)
    merged = gate_ref[:, :d].astype(F32) * up_f + gate_ref[:, d:].astype(F32) * up_d
    x1 = x_ref[...] + jnp.dot(merged.astype(BF16), wo_ref[...], preferred_element_type=F32)
    x1_ref[...] = x1
    h2 = _rms(x1, gn_ref[...])
    h2_ref[...] = h2

    logits = jnp.dot(h2, wr_ref[...], precision=HI, preferred_element_type=F32) + br_ref[...]
    lane = lax.broadcasted_iota(I32, logits.shape, 1).astype(F32)
    out_lane = lax.broadcasted_iota(I32, (tm, TOP_K), 1)
    remaining = logits
    picks, vals = [], []
    for _ in range(TOP_K):
        mx = jnp.max(remaining, axis=1, keepdims=True)
        sel = jnp.min(jnp.where(remaining == mx, lane, float(n_experts)), axis=1, keepdims=True)
        hit = lane == sel
        picks.append(hit)
        vals.append(mx)
        remaining = jnp.where(hit, NEG_INF, remaining)
    exps = [jnp.exp(v - vals[0]) for v in vals]
    denom = exps[0] + exps[1] + exps[2] + exps[3]

    @pl.when(i == 0)
    def _():
        cnt_acc[...] = base_ref[...]

    chosen = jnp.zeros(logits.shape, F32)
    for hit in picks:
        chosen = chosen + jnp.where(hit, 1.0, 0.0)
    row = lax.broadcasted_iota(I32, (tm, tm), 0)
    col = lax.broadcasted_iota(I32, (tm, tm), 1)
    strict = jnp.where(col < row, 1.0, 0.0).astype(BF16)
    before = jnp.dot(strict, chosen.astype(BF16), preferred_element_type=F32) + cnt_acc[...]
    cnt_acc[...] = cnt_acc[...] + jnp.sum(chosen, axis=0, keepdims=True)
    cnt_ref[...] = cnt_acc[...]

    idx_out = jnp.zeros((tm, TOP_K), I32)
    wgt_out = jnp.zeros((tm, TOP_K), F32)
    rank_out = jnp.zeros((tm, TOP_K), I32)
    for kk in range(TOP_K):
        sel = jnp.sum(jnp.where(picks[kk], lane, 0.0), axis=1, keepdims=True).astype(I32)
        rk = jnp.sum(jnp.where(picks[kk], before, 0.0), axis=1, keepdims=True).astype(I32)
        idx_out = jnp.where(out_lane == kk, sel, idx_out)
        wgt_out = jnp.where(out_lane == kk, exps[kk] / denom, wgt_out)
        rank_out = jnp.where(out_lane == kk, rk, rank_out)
    idx_ref[...] = idx_out
    wgt_ref[...] = wgt_out
    rank_ref[...] = rank_out


def _mix(o, gates, x, wuf, wud, wo, g_norm, w_router, b_router, base, *, tm):
    m, d = x.shape
    n_exp = w_router.shape[1]
    row_spec = lambda n: pl.BlockSpec((tm, n), lambda i: (i, 0))
    return pl.pallas_call(
        functools.partial(_mix_kernel, n_experts=n_exp),
        grid=(m // tm,),
        in_specs=[row_spec(o.shape[1]), row_spec(gates.shape[1]), row_spec(d),
                  _const_spec(wuf.shape), _const_spec(wud.shape), _const_spec(wo.shape),
                  _const_spec((1, d)), _const_spec(w_router.shape), _const_spec((1, n_exp)),
                  _const_spec((1, n_exp))],
        out_specs=(row_spec(d), row_spec(d), row_spec(TOP_K), row_spec(TOP_K), row_spec(TOP_K),
                   pl.BlockSpec((1, n_exp), lambda i: (0, 0))),
        out_shape=(jax.ShapeDtypeStruct((m, d), F32), jax.ShapeDtypeStruct((m, d), F32),
                   jax.ShapeDtypeStruct((m, TOP_K), I32), jax.ShapeDtypeStruct((m, TOP_K), F32),
                   jax.ShapeDtypeStruct((m, TOP_K), I32), jax.ShapeDtypeStruct((1, n_exp), F32)),
        scratch_shapes=[pltpu.VMEM((1, n_exp), F32)],
        compiler_params=_cparams(1),
        name="mix_router",
    )(o, gates, x, wuf, wud, wo, g_norm, w_router, b_router, base)


def _row_copy(src, src_row, dst, dst_row, sem):
    return pltpu.make_async_copy(src.at[pl.ds(src_row, 1), :], dst.at[pl.ds(dst_row, 1), :], sem)


def _scatter_rows(dest_ref, h_ref, rows_ref, sem, tc):
    def issue(t, carry):
        for kk in range(TOP_K):
            _row_copy(h_ref, t, rows_ref, dest_ref[t * TOP_K + kk], sem).start()
        return carry

    lax.fori_loop(0, tc, issue, 0, unroll=8)
    for kk in range(TOP_K):
        pltpu.make_async_copy(h_ref, rows_ref.at[pl.ds(0, tc), :], sem).wait()


def _dispatch_kernel(tail_ref, na_ref, dest_a_ref, dest_b_ref, ha_ref, hb_ref, rows_ref,
                     zero_buf, sem, zero_sem, *, tc_a, tc_b):
    n_blocks = rows_ref.shape[0] // ROW_BLOCK
    i = pl.program_id(0)
    last = pl.num_programs(0) - 1

    @pl.when(i == 0)
    def _():
        zero_buf[...] = jnp.zeros_like(zero_buf)

        def zero_copy(start):
            return pltpu.make_async_copy(zero_buf, rows_ref.at[pl.ds(pl.multiple_of(start, ROW_BLOCK), ROW_BLOCK), :],
                                         zero_sem)

        def start_block(blk, carry):
            zero_copy(blk * ROW_BLOCK).start()
            return carry

        def wait_block(blk, carry):
            zero_copy(blk * ROW_BLOCK).wait()
            return carry

        for e in range(tail_ref.shape[0]):
            @pl.when(tail_ref[e] >= 0)
            def _():
                zero_copy(tail_ref[e]).start()
        lax.fori_loop(na_ref[0], n_blocks, start_block, 0)
        for e in range(tail_ref.shape[0]):
            @pl.when(tail_ref[e] >= 0)
            def _():
                zero_copy(tail_ref[e]).wait()
        lax.fori_loop(na_ref[0], n_blocks, wait_block, 0)

    @pl.when(i < last)
    def _():
        _scatter_rows(dest_a_ref, ha_ref, rows_ref, sem, tc_a)

    @pl.when(i == last)
    def _():
        _scatter_rows(dest_b_ref, hb_ref, rows_ref, sem, tc_b)


def _dispatch(tail_start, n_active, dest_a, h_a, dest_b, h_b, n_rows, *, tc_a):
    m_a, d = h_a.shape
    tc_b = h_b.shape[0]
    n_a = m_a // tc_a
    blk_a = lambda i, tail, na: (jnp.minimum(i, n_a - 1),)
    grid_spec = pltpu.PrefetchScalarGridSpec(
        num_scalar_prefetch=2,
        grid=(n_a + 1,),
        in_specs=[pl.BlockSpec((tc_a * TOP_K,), blk_a, memory_space=pltpu.SMEM),
                  pl.BlockSpec((tc_b * TOP_K,), lambda i, tail, na: (0,), memory_space=pltpu.SMEM),
                  pl.BlockSpec((tc_a, d), lambda i, tail, na: blk_a(i, tail, na) + (0,)),
                  pl.BlockSpec((tc_b, d), lambda i, tail, na: (0, 0))],
        out_specs=pl.BlockSpec(memory_space=pl.ANY),
        scratch_shapes=[pltpu.VMEM((ROW_BLOCK, d), F32), pltpu.SemaphoreType.DMA(()),
                        pltpu.SemaphoreType.DMA(())],
    )
    return pl.pallas_call(
        functools.partial(_dispatch_kernel, tc_a=tc_a, tc_b=tc_b),
        grid_spec=grid_spec,
        out_shape=jax.ShapeDtypeStruct((n_rows, d), F32),
        compiler_params=_cparams(1, has_side_effects=True),
        name="dispatch",
    )(tail_start, n_active, dest_a, dest_b, h_a, h_b)


def _combine_kernel(dest_ref, y_ref, w_ref, x1_ref, g_ref, o_ref, buf, sem, *, tc, final_norm):
    n_parts = 2
    th = tc // n_parts
    for part in range(n_parts):
        def issue(t, carry, part=part):
            for kk in range(TOP_K):
                _row_copy(y_ref, dest_ref[t * TOP_K + kk], buf.at[kk], t, sem.at[part]).start()
            return carry

        lax.fori_loop(part * th, (part + 1) * th, issue, 0, unroll=8)
    for part in range(n_parts):
        rows = pl.ds(part * th, th)
        for kk in range(TOP_K):
            pltpu.make_async_copy(y_ref.at[pl.ds(0, th), :], buf.at[kk, rows, :], sem.at[part]).wait()
        out = x1_ref[rows, :]
        for kk in range(TOP_K):
            out = out + buf[kk, rows, :] * w_ref[rows, kk:kk + 1]
        if final_norm:
            out = _rms(out, g_ref[...])
        o_ref[rows, :] = out


def _combine(dest_flat, y_rows, wgt, x1, g, *, tc, final_norm):
    m, d = x1.shape
    return pl.pallas_call(
        functools.partial(_combine_kernel, tc=tc, final_norm=final_norm),
        grid=(m // tc,),
        in_specs=[pl.BlockSpec((tc * TOP_K,), lambda i: (i,), memory_space=pltpu.SMEM),
                  pl.BlockSpec(memory_space=pl.ANY),
                  pl.BlockSpec((tc, TOP_K), lambda i: (i, 0)),
                  pl.BlockSpec((tc, d), lambda i: (i, 0)),
                  pl.BlockSpec((1, d), lambda i: (0, 0))],
        out_specs=pl.BlockSpec((tc, d), lambda i: (i, 0)),
        out_shape=jax.ShapeDtypeStruct((m, d), F32),
        scratch_shapes=[pltpu.VMEM((TOP_K, tc, d), F32), pltpu.SemaphoreType.DMA((2,))],
        compiler_params=_cparams(1),
        name="combine",
    )(dest_flat, y_rows, wgt, x1, g)


def _expert_kernel(be_ref, na_ref, x_ref, wgu_ref, bgu_ref, wd_ref, bd_ref, y_ref, wgu_bf, wd_bf):
    i = pl.program_id(0)
    d_ff = wd_ref.shape[0]
    active = i < na_ref[0]
    new_expert = jnp.logical_or(i == 0, be_ref[i] != be_ref[jnp.maximum(i - 1, 0)])

    @pl.when(jnp.logical_and(active, new_expert))
    def _():
        wgu_bf[...] = wgu_ref[...].astype(BF16)
        wd_bf[...] = wd_ref[...].astype(BF16)

    @pl.when(active)
    def _():
        gu = jnp.dot(x_ref[...].astype(BF16), wgu_bf[...], preferred_element_type=F32) + bgu_ref[...]
        glu = jnp.minimum(gu[:, :d_ff], SWIGLU_LIMIT)
        lin = jnp.clip(gu[:, d_ff:], -SWIGLU_LIMIT, SWIGLU_LIMIT)
        act = glu * (1.0 / (1.0 + jnp.exp(-SWIGLU_ALPHA * glu))) * (lin + 1.0)
        y_ref[...] = jnp.dot(act.astype(BF16), wd_bf[...], preferred_element_type=F32) + bd_ref[...]

    @pl.when(i >= na_ref[0])
    def _():
        y_ref[...] = jnp.zeros_like(y_ref)


def _experts(block_expert, n_active, x_rows, wgu, bgu, wd, bd):
    n_rows, d = x_rows.shape
    n_blocks = n_rows // ROW_BLOCK
    grid_spec = pltpu.PrefetchScalarGridSpec(
        num_scalar_prefetch=2,
        grid=(n_blocks,),
        in_specs=[pl.BlockSpec((ROW_BLOCK, d), lambda i, be, na: (jnp.minimum(i, na[0] - 1), 0)),
                  pl.BlockSpec((None,) + wgu.shape[1:], lambda i, be, na: (be[i], 0, 0)),
                  pl.BlockSpec((None,) + bgu.shape[1:], lambda i, be, na: (be[i], 0, 0)),
                  pl.BlockSpec((None,) + wd.shape[1:], lambda i, be, na: (be[i], 0, 0)),
                  pl.BlockSpec((None,) + bd.shape[1:], lambda i, be, na: (be[i], 0, 0))],
        out_specs=pl.BlockSpec((ROW_BLOCK, d), lambda i, be, na: (i, 0)),
        scratch_shapes=[pltpu.VMEM(wgu.shape[1:], BF16), pltpu.VMEM(wd.shape[1:], BF16)],
    )
    return pl.pallas_call(
        _expert_kernel,
        grid_spec=grid_spec,
        out_shape=jax.ShapeDtypeStruct((n_rows, d), F32),
        compiler_params=_cparams(1),
        name="experts",
    )(block_expert, n_active, x_rows, wgu, bgu, wd, bd)


def kernel(x_prompt, x_sample, cache_fox_k, cache_fox_v, cache_fox_logf, cache_diff_k, cache_diff_v,
           page_table, g_attn_norm, w_in, b_forget, b_gate, lambda_q1, lambda_k1, lambda_q2, lambda_k2,
           g_subln, w_up_fox, w_up_diff, w_out, g_ffn_norm, w_router, b_router, w_gate_up, b_gate_up,
           w_down, b_down, g_final):
    depth = w_in.shape[0]
    b, t, d = x_prompt.shape
    n_seq = x_sample.shape[0]
    assert x_sample.shape[1] == 1
    n_phys, page, n_fox = cache_fox_k.shape[1:4]
    n_diff = cache_diff_k.shape[3]
    width = n_fox * HEAD_DIM
    assert 2 * n_diff == n_fox and cache_diff_v.shape[4] == 2 * HEAD_DIM
    n_exp = w_router.shape[2]
    m_p = b * t
    slopes = 2.0 ** (-8.0 * jnp.arange(1, n_diff + 1, dtype=F32) / n_diff)
    slope_l2 = (jnp.repeat(slopes, 2) * LOG2E)
    slope_col = jnp.broadcast_to(slope_l2[:, None], (n_fox, LANES))
    pq, pk = _bias_placement(2 * n_fox)
    tri_u = (jnp.arange(page)[:, None] <= jnp.arange(page)[None, :]).astype(BF16)
    tm_p = min(TOKEN_BLOCK, t)

    xp = x_prompt.reshape(m_p, d)
    xs = x_sample.reshape(n_seq, d)
    outs_p, outs_s = [], []
    for l in range(depth):
        lam_init = 0.8 - 0.6 * math.exp(-0.3 * l)
        lam = (jnp.exp(jnp.sum(lambda_q1[l].astype(F32) * lambda_k1[l].astype(F32)))
               - jnp.exp(jnp.sum(lambda_q2[l].astype(F32) * lambda_k2[l].astype(F32)))
               + lam_init).reshape(1)
        out_scale = 1.0 - lam_init
        w = w_in[l]
        o_fq, o_fk, o_fv, o_zf = 0, width, 2 * width, 3 * width
        o_dq = o_zf + n_fox
        o_dk, o_dv, o_zg = o_dq + width, o_dq + 2 * width, o_dq + 3 * width
        sl = lambda o, n=width: w[:, o:o + n]
        w_nn = jnp.concatenate([sl(o_fq), sl(o_dq), sl(o_fk), sl(o_dk), sl(o_dv), w[:, o_zg:]],
                               axis=1).astype(BF16)
        w_nt = jnp.concatenate([sl(o_fk), sl(o_dk), sl(o_fv), sl(o_dv)], axis=1).T.astype(BF16)
        w_f = sl(o_zf, n_fox)
        g_attn = g_attn_norm[l][None]
        b_f = b_forget[l][None]
        b_g = b_gate[l][None]
        gs = g_subln[l][None]
        wuf, wud, wo = (w_up_fox[l].astype(BF16), w_up_diff[l].astype(BF16), w_out[l].astype(BF16))
        g_ffn = g_ffn_norm[l][None]
        last = l == depth - 1
        g_out = g_final[None] if last else jnp.ones((1, d), F32)
        proj = functools.partial(_proj, g=g_attn, w_nn=w_nn, w_nt=w_nt, w_f=w_f, b_f=b_f, b_g=b_g,
                                 slope_l2=slope_l2[None], pq=pq, pk=pk)

        (q_p, qb_p, k_p, kb_p, vt_p, fkt_p, fvt_p, dkt_p, dv_p, lft_p, gate_p) = proj(
            xp, tm=tm_p, n_batch=b)
        r3 = lambda a: a.reshape(b, t, a.shape[-1])
        o_p = _prompt_attention(lam, r3(q_p), r3(qb_p), r3(k_p), r3(kb_p), vt_p, gs,
                                blk=tm_p, n_fox_units=n_fox // 2, out_scale=out_scale)
        x1_p, h2_p, idx_p, wgt_p, rank_p, cnt_p = _mix(
            o_p.reshape(m_p, 2 * width), gate_p, xp, wuf, wud, wo, g_ffn, w_router[l], b_router[l][None],
            jnp.zeros((1, n_exp), F32), tm=tm_p)

        (q_s, _, _, _, _, fkt_s, fvt_s, dkt_s, dv_s, lft_s, gate_s) = proj(xs, tm=n_seq, n_batch=1)
        caches = (jnp.transpose(cache_fox_k[l], (0, 2, 3, 1)).reshape(n_phys, width, page),
                  jnp.transpose(cache_fox_v[l], (0, 2, 3, 1)).reshape(n_phys, width, page),
                  jnp.transpose(cache_fox_logf[l], (0, 2, 1)),
                  jnp.transpose(cache_diff_k[l], (0, 2, 3, 4, 1)).reshape(n_phys, width, page),
                  cache_diff_v[l].reshape(n_phys, page * n_diff, 2 * HEAD_DIM))
        q_t = q_s.astype(F32).T
        oft, od = _decode_attention(
            page_table, lam, caches, q_t[:width], q_t[width:],
            (fkt_s[0], fvt_s[0], lft_s[0], dkt_s[0]), dv_s.reshape(n_seq, n_diff, 2 * HEAD_DIM),
            slope_col, tri_u, gs, pages_per_step=DECODE_PAGES, out_scale=out_scale)
        o_s = jnp.concatenate([oft.T, od.reshape(n_seq, width)], axis=1).astype(BF16)
        x1_s, h2_s, idx_s, wgt_s, rank_s, cnt_all = _mix(
            o_s, gate_s, xs, wuf, wud, wo, g_ffn, w_router[l], b_router[l][None], cnt_p, tm=n_seq)

        counts = cnt_all[0].astype(I32)
        padded = (counts + ROW_BLOCK - 1) // ROW_BLOCK * ROW_BLOCK
        pad_end = jnp.cumsum(padded)
        pad_start = pad_end - padded
        n_assign = (m_p + n_seq) * TOP_K
        n_blocks = -(-n_assign // ROW_BLOCK) + n_exp
        block_start = jnp.arange(n_blocks, dtype=I32) * ROW_BLOCK
        block_expert = jnp.minimum(jnp.sum((pad_end[None, :] <= block_start[:, None]).astype(I32), axis=1),
                                   n_exp - 1).astype(I32)
        n_active = (pad_end[-1:] // ROW_BLOCK).astype(I32)
        dest_p = (pad_start[idx_p] + rank_p).reshape(-1)
        dest_s = (pad_start[idx_s] + rank_s).reshape(-1)

        tail_start = jnp.where(padded > 0, pad_end - ROW_BLOCK, -1).astype(I32)
        x_rows = _dispatch(tail_start, n_active, dest_p, h2_p, dest_s, h2_s, n_blocks * ROW_BLOCK,
                           tc_a=2 * ROUTE_BLOCK)
        y_rows = _experts(block_expert, n_active, x_rows, w_gate_up[l], b_gate_up[l][:, None],
                          w_down[l], b_down[l][:, None])
        xp = _combine(dest_p, y_rows, wgt_p, x1_p, g_out, tc=ROUTE_BLOCK, final_norm=last)
        xs = _combine(dest_s, y_rows, wgt_s, x1_s, g_out, tc=ROUTE_BLOCK, final_norm=last)

        def kv_rows(kt_, n_b, n_t):
            return jnp.transpose(kt_.reshape(n_b, -1, HEAD_DIM, n_t), (0, 3, 1, 2))

        def cache_rows(fkt_, fvt_, lft_, dkt_, dv_, n_b, n_t):
            return (kv_rows(fkt_, n_b, n_t), kv_rows(fvt_, n_b, n_t), jnp.transpose(lft_, (0, 2, 1)),
                    kv_rows(dkt_, n_b, n_t).reshape(n_b, n_t, n_diff, 2, HEAD_DIM),
                    dv_.reshape(n_b, n_t, n_diff, 2 * HEAD_DIM))

        outs_p.append(cache_rows(fkt_p, fvt_p, lft_p, dkt_p, dv_p, b, t))
        rows_s = cache_rows(fkt_s, fvt_s, lft_s, dkt_s, dv_s, 1, n_seq)
        outs_s.append(tuple(a.reshape((n_seq, 1) + a.shape[2:]) for a in rows_s))

    stack = lambda rows: [jnp.stack(r) for r in zip(*rows)]
    return (xp.reshape(b, t, d), xs.reshape(n_seq, 1, d), *stack(outs_p), *stack(outs_s))
```
